```python
import jax, jax.numpy as jnp
from jax import lax
import numpy as np

D_MODEL = 1024
BATCH = 8
SEQ = 2048
DEPTH = 2

N_HEADS = 16
HEAD_DIM = D_MODEL // N_HEADS
Q_BLOCK = 128
N_A_LAYERS = DEPTH // 2
N_B_LAYERS = DEPTH - N_A_LAYERS
N_GROUPS = 4
EXPERTS_PER_GROUP = 4
N_EXPERTS = N_GROUPS * EXPERTS_PER_GROUP
TOP_K_IN_GROUP = 2
D_EXPERT = D_MODEL // 2
RMS_EPS = 1e-6

kernel_name = "yoco_stickbreak_fox_hmoe"


def rms_norm(x, g):
    x32 = x.astype(jnp.float32)
    y = x32 * lax.rsqrt(jnp.mean(x32 * x32, axis=-1, keepdims=True) + RMS_EPS)
    return (y * g.astype(jnp.float32)).astype(x.dtype)


def split_heads(t):
    b, s, _ = t.shape
    return t.reshape(b, s, N_HEADS, HEAD_DIM).transpose(0, 2, 1, 3)


def merge_heads(t):
    b, h, s, dh = t.shape
    return t.transpose(0, 2, 1, 3).reshape(b, s, h * dh)


def stick_breaking_attention(q, k, v):
    seq = q.shape[2]
    scale = HEAD_DIM ** -0.5
    outs = []
    for i in range(seq // Q_BLOCK):
        q0 = i * Q_BLOCK
        kv_len = q0 + Q_BLOCK
        qb = q[:, :, q0:kv_len].astype(jnp.float32)
        kb = k[:, :, :kv_len].astype(jnp.float32)
        vb = v[:, :, :kv_len].astype(jnp.float32)
        z = jnp.einsum('bhqd,bhkd->bhqk', qb, kb) * scale
        t_pos = q0 + jnp.arange(Q_BLOCK)[:, None]
        s_pos = jnp.arange(kv_len)[None, :]
        strict = s_pos < t_pos
        log_keep = jnp.where(strict, jax.nn.log_sigmoid(-z), 0.0)
        log_after = lax.cumsum(log_keep, axis=3, reverse=True) - log_keep
        w = jnp.where(strict, jnp.exp(jax.nn.log_sigmoid(z) + log_after), 0.0)
        outs.append(jnp.einsum('bhqk,bhkd->bhqd', w, vb))
    return jnp.concatenate(outs, axis=2).astype(q.dtype)


def forgetting_attention(q, k, v, cum_logf):
    seq = q.shape[2]
    scale = HEAD_DIM ** -0.5
    outs = []
    for i in range(seq // Q_BLOCK):
        q0 = i * Q_BLOCK
        kv_len = q0 + Q_BLOCK
        qb = q[:, :, q0:kv_len].astype(jnp.float32)
        kb = k[:, :, :kv_len].astype(jnp.float32)
        vb = v[:, :, :kv_len].astype(jnp.float32)
        logits = (jnp.einsum('bhqd,bhkd->bhqk', qb, kb) * scale
                  + cum_logf[:, :, q0:kv_len, None] - cum_logf[:, :, None, :kv_len])
        t_pos = q0 + jnp.arange(Q_BLOCK)[:, None]
        s_pos = jnp.arange(kv_len)[None, :]
        probs = jax.nn.softmax(jnp.where(s_pos <= t_pos, logits, -jnp.inf), axis=-1)
        outs.append(jnp.einsum('bhqk,bhkd->bhqd', probs, vb))
    return jnp.concatenate(outs, axis=2).astype(q.dtype)


def hierarchical_moe(x, w_group, w_router, w_gate, w_up, w_down):
    b, s, d = x.shape
    xt = x.reshape(b * s, d)
    group_probs = jax.nn.softmax((xt @ w_group).astype(jnp.float32), axis=-1)
    g_prob, g_idx = lax.top_k(group_probs, 1)
    expert_logits = (xt @ w_router).astype(jnp.float32).reshape(-1, N_GROUPS, EXPERTS_PER_GROUP)
    in_group = jnp.take_along_axis(expert_logits, g_idx[:, :, None], axis=1)[:, 0]
    top_logits, e_idx = lax.top_k(in_group, TOP_K_IN_GROUP)
    e_w = jax.nn.softmax(top_logits, axis=-1) * g_prob
    expert_id = g_idx * EXPERTS_PER_GROUP + e_idx
    combine = jnp.sum(jax.nn.one_hot(expert_id, N_EXPERTS, dtype=jnp.float32) * e_w[..., None], axis=1)
    h = jax.nn.silu(jnp.einsum('nd,edf->nef', xt, w_gate)) * jnp.einsum('nd,edf->nef', xt, w_up)
    h = h * combine[:, :, None].astype(h.dtype)
    y = jnp.einsum('nef,efd->nd', h, w_down)
    return y.reshape(b, s, d).astype(x.dtype)


def setup_inputs(seed: int = 0) -> dict:
    key = jax.random.key(seed)
    ks = jax.random.split(key, 20)
    d, h, f = D_MODEL, N_HEADS, D_EXPERT
    sd = d ** -0.5
    nrm = jax.random.normal
    return {
        "x": nrm(ks[0], (BATCH, SEQ, d), jnp.float32),
        "attn_norm": 1.0 + 0.02 * nrm(ks[1], (DEPTH, d), jnp.float32),
        "w_qkv_a": nrm(ks[2], (N_A_LAYERS, d, 3 * d), jnp.float32) * sd,
        "w_o_a": nrm(ks[3], (N_A_LAYERS, d, d), jnp.float32) * sd,
        "kv_norm": 1.0 + 0.02 * nrm(ks[4], (d,), jnp.float32),
        "w_kvf": nrm(ks[5], (d, 2 * d + h), jnp.float32) * sd,
        "b_f": jax.random.uniform(ks[6], (h,), jnp.float32, 1.0, 4.0),
        "w_q_b": nrm(ks[7], (N_B_LAYERS, d, d), jnp.float32) * sd,
        "w_o_b": nrm(ks[8], (N_B_LAYERS, d, d), jnp.float32) * sd,
        "moe_norm": 1.0 + 0.02 * nrm(ks[9], (DEPTH, d), jnp.float32),
        "w_group": nrm(ks[10], (DEPTH, d, N_GROUPS), jnp.float32) * sd,
        "w_router": nrm(ks[11], (DEPTH, d, N_EXPERTS), jnp.float32) * sd,
        "w_gate": nrm(ks[12], (DEPTH, N_EXPERTS, d, f), jnp.float32) * sd,
        "w_up": nrm(ks[13], (DEPTH, N_EXPERTS, d, f), jnp.float32) * sd,
        "w_down": nrm(ks[14], (DEPTH, N_EXPERTS, f, d), jnp.float32) * (f ** -0.5),
        "final_norm": 1.0 + 0.02 * nrm(ks[15], (d,), jnp.float32),
    }


def reference(x, attn_norm, w_qkv_a, w_o_a, kv_norm, w_kvf, b_f, w_q_b, w_o_b,
              moe_norm, w_group, w_router, w_gate, w_up, w_down, final_norm):
    d = D_MODEL
    h = x
    k_shared = v_shared = cum_logf = None
    for layer in range(DEPTH):
        if layer < N_A_LAYERS:
            hn = rms_norm(h, attn_norm[layer])
            q, k, v = jnp.split(hn @ w_qkv_a[layer], 3, axis=-1)
            mix = stick_breaking_attention(split_heads(q), split_heads(k), split_heads(v))
            h = h + merge_heads(mix) @ w_o_a[layer]
        else:
            j = layer - N_A_LAYERS
            if j == 0:
                u = rms_norm(h, kv_norm)
                kvf = u @ w_kvf
                k_shared = split_heads(kvf[..., :d])
                v_shared = split_heads(kvf[..., d:2 * d])
                log_f = jax.nn.log_sigmoid((kvf[..., 2 * d:] + b_f).astype(jnp.float32))
                cum_logf = lax.cumsum(log_f, axis=1).transpose(0, 2, 1)
            hn = rms_norm(h, attn_norm[layer])
            q = split_heads(hn @ w_q_b[j])
            mix = forgetting_attention(q, k_shared, v_shared, cum_logf)
            h = h + merge_heads(mix) @ w_o_b[j]
        h = h + hierarchical_moe(rms_norm(h, moe_norm[layer]), w_group[layer], w_router[layer],
                                 w_gate[layer], w_up[layer], w_down[layer])
    return rms_norm(h, final_norm)
```

```python
import functools

import jax
import jax.numpy as jnp
from jax import lax
from jax.experimental import pallas as pl
from jax.experimental.pallas import tpu as pltpu

D_MODEL = 1024
N_HEADS = 16
HEAD_DIM = 64
N_GROUPS = 4
EXPERTS_PER_GROUP = 4
N_EXPERTS = 16
D_EXPERT = 512
RMS_EPS = 1e-6

LANES = 128
HEAD_PAIRS = N_HEADS // 2
ROW_TILE = 512
ATTN_TILE = 256
VMEM_LIMIT = 48 * 1024 * 1024

F32 = jnp.float32
BF16 = jnp.bfloat16
NT_DIMS = (((1,), (1,)), ((), ()))


def _params(*semantics):
    return pltpu.CompilerParams(dimension_semantics=semantics, vmem_limit_bytes=VMEM_LIMIT)


def _rms_normed(x, gain):
    var = jnp.mean(x * x, axis=-1, keepdims=True)
    return x * lax.rsqrt(var + RMS_EPS) * gain


def _norm_matmul_kernel(h_ref, g_ref, w_ref, o_ref):
    xn = _rms_normed(h_ref[...], g_ref[...]).astype(BF16)
    o_ref[...] = jnp.dot(xn, w_ref[...], preferred_element_type=F32).astype(o_ref.dtype)


def _norm_matmul(h, gain, w, col_tile):
    n, d = h.shape
    n_out = w.shape[1]
    return pl.pallas_call(
        _norm_matmul_kernel,
        grid=(n_out // col_tile, n // ROW_TILE),
        in_specs=[
            pl.BlockSpec((ROW_TILE, d), lambda j, i: (i, 0)),
            pl.BlockSpec((1, d), lambda j, i: (0, 0)),
            pl.BlockSpec((d, col_tile), lambda j, i: (0, j)),
        ],
        out_specs=pl.BlockSpec((ROW_TILE, col_tile), lambda j, i: (i, j)),
        out_shape=jax.ShapeDtypeStruct((n, n_out), BF16),
        compiler_params=_params("parallel", "arbitrary"),
        name="norm_matmul",
    )(h, gain.reshape(1, d), w)


def _matmul_residual_kernel(mix_ref, w_ref, h_ref, o_ref):
    o_ref[...] = h_ref[...] + jnp.dot(mix_ref[...], w_ref[...], preferred_element_type=F32)


def _matmul_residual(mix, w, h):
    n, d = h.shape
    return pl.pallas_call(
        _matmul_residual_kernel,
        grid=(n // ROW_TILE,),
        in_specs=[
            pl.BlockSpec((ROW_TILE, d), lambda i: (i, 0)),
            pl.BlockSpec((d, d), lambda i: (0, 0)),
            pl.BlockSpec((ROW_TILE, d), lambda i: (i, 0)),
        ],
        out_specs=pl.BlockSpec((ROW_TILE, d), lambda i: (i, 0)),
        out_shape=jax.ShapeDtypeStruct((n, d), F32),
        compiler_params=_params("parallel"),
        name="matmul_residual",
    )(mix, w, h)


def _head_masks(rows):
    lane = lax.broadcasted_iota(jnp.int32, (rows, LANES), 1)
    return lane < HEAD_DIM


def _split_heads(x, first_head):
    zero = jnp.zeros_like(x)
    return jnp.where(first_head, x, zero), jnp.where(first_head, zero, x)


def _sb_attn_kernel(q_ref, k_ref, v_ref, o_ref):
    t = ATTN_TILE
    i = pl.program_id(2)
    first_head = _head_masks(t)
    q_heads = _split_heads(q_ref[0], first_head)
    row = lax.broadcasted_iota(jnp.int32, (t, t), 0)
    col = lax.broadcasted_iota(jnp.int32, (t, t), 1)
    strict = col < row
    suffix_ones = (row >= col).astype(BF16)

    def block(j, carry, diagonal):
        c0, c1, acc = carry
        start = pl.multiple_of(j * t, t)
        kb = k_ref[0, pl.ds(start, t), :]
        v_heads = _split_heads(v_ref[0, pl.ds(start, t), :], first_head)
        new_c = []
        for qh, vh, c in zip(q_heads, v_heads, (c0, c1)):
            z = lax.dot_general(qh, kb, NT_DIMS, preferred_element_type=F32)
            log_keep = -(jnp.maximum(z, 0.0) + jnp.log(1.0 + jnp.exp(-jnp.abs(z))))
            if diagonal:
                log_keep = jnp.where(strict, log_keep, 0.0)
            suffix = jnp.dot(log_keep.astype(BF16), suffix_ones, preferred_element_type=F32)
            w = jnp.exp(z + suffix + c)
            if diagonal:
                w = jnp.where(strict, w, 0.0)
            acc = acc + jnp.dot(w.astype(BF16), vh, preferred_element_type=F32)
            new_c.append(c + jnp.sum(log_keep, axis=1, keepdims=True))
        return new_c[0], new_c[1], acc

    zeros = jnp.zeros((t, 1), F32)
    carry = block(i, (zeros, zeros, jnp.zeros((t, LANES), F32)), True)
    carry = lax.fori_loop(0, i, lambda n, c: block(i - 1 - n, c, False), carry)
    o_ref[0] = carry[2].astype(o_ref.dtype)


def _sb_attention(qkv, batch, seq):
    t = ATTN_TILE
    k_off = D_MODEL // LANES
    return pl.pallas_call(
        _sb_attn_kernel,
        grid=(batch, HEAD_PAIRS, seq // t),
        in_specs=[
            pl.BlockSpec((1, t, LANES), lambda b, p, i: (b, i, p)),
            pl.BlockSpec((1, seq, LANES), lambda b, p, i: (b, 0, k_off + p)),
            pl.BlockSpec((1, seq, LANES), lambda b, p, i: (b, 0, 2 * k_off + p)),
        ],
        out_specs=pl.BlockSpec((1, t, LANES), lambda b, p, i: (b, i, p)),
        out_shape=jax.ShapeDtypeStruct((batch, seq, D_MODEL), BF16),
        compiler_params=_params("parallel", "parallel", "arbitrary"),
        name="sb_attention",
    )(qkv, qkv, qkv)


def _forget_gate_kernel(h_ref, g_ref, wf_ref, bf_ref, o_ref):
    t = ATTN_TILE
    seq = h_ref.shape[1]
    u = _rms_normed(h_ref[0], g_ref[...]).astype(BF16)
    logits = lax.dot_general(wf_ref[...], u, NT_DIMS, preferred_element_type=F32) + bf_ref[...]
    log_f = jnp.minimum(logits, 0.0) - jnp.log(1.0 + jnp.exp(-jnp.abs(logits)))
    row = lax.broadcasted_iota(jnp.int32, (t, t), 0)
    col = lax.broadcasted_iota(jnp.int32, (t, t), 1)
    prefix_ones = (row <= col).astype(BF16)
    carry = jnp.zeros((N_HEADS, 1), F32)
    for n in range(seq // t):
        blk = log_f[:, n * t:(n + 1) * t]
        hi = blk.astype(BF16)
        lo = (blk - hi.astype(F32)).astype(BF16)
        cs = (jnp.dot(hi, prefix_ones, preferred_element_type=F32)
              + jnp.dot(lo, prefix_ones, preferred_element_type=F32) + carry)
        o_ref[0, :, n * t:(n + 1) * t] = cs
        carry = cs[:, t - 1:t]


def _forget_gates(h3, gain, w_f_t, b_f):
    batch, seq, d = h3.shape
    return pl.pallas_call(
        _forget_gate_kernel,
        grid=(batch,),
        in_specs=[
            pl.BlockSpec((1, seq, d), lambda b: (b, 0, 0)),
            pl.BlockSpec((1, d), lambda b: (0, 0)),
            pl.BlockSpec((N_HEADS, d), lambda b: (0, 0)),
            pl.BlockSpec((N_HEADS, 1), lambda b: (0, 0)),
        ],
        out_specs=pl.BlockSpec((1, N_HEADS, seq), lambda b: (b, 0, 0)),
        out_shape=jax.ShapeDtypeStruct((batch, N_HEADS, seq), F32),
        compiler_params=_params("parallel"),
        name="forget_gates",
    )(h3, gain.reshape(1, d), w_f_t, b_f.reshape(N_HEADS, 1))


def _fox_attn_kernel(q_ref, k_ref, v_ref, c_ref, o_ref):
    t = ATTN_TILE
    i = pl.program_id(2)
    first_head = _head_masks(t)
    q_heads = _split_heads(q_ref[0], first_head)
    row = lax.broadcasted_iota(jnp.int32, (t, t), 0)
    col = lax.broadcasted_iota(jnp.int32, (t, t), 1)
    causal = col <= row

    def block(j, carry, diagonal):
        m0, m1, l0, l1, acc = carry
        start = pl.multiple_of(j * t, t)
        kb = k_ref[0, pl.ds(start, t), :]
        v_heads = _split_heads(v_ref[0, pl.ds(start, t), :], first_head)
        cb = c_ref[0, 0, j]
        new_m, new_l, alphas, pv = [], [], [], []
        for h, (qh, vh, m, l) in enumerate(zip(q_heads, v_heads, (m0, m1), (l0, l1))):
            s = lax.dot_general(qh, kb, NT_DIMS, preferred_element_type=F32) - cb[h:h + 1, :]
            if diagonal:
                s = jnp.where(causal, s, -jnp.inf)
            m_new = jnp.maximum(m, jnp.max(s, axis=1, keepdims=True))
            p = jnp.exp(s - m_new)
            alpha = jnp.exp(m - m_new)
            new_m.append(m_new)
            new_l.append(alpha * l + jnp.sum(p, axis=1, keepdims=True))
            alphas.append(alpha)
            pv.append(jnp.dot(p.astype(BF16), vh, preferred_element_type=F32))
        acc = acc * jnp.where(first_head, alphas[0], alphas[1]) + pv[0] + pv[1]
        return new_m[0], new_m[1], new_l[0], new_l[1], acc

    neg = jnp.full((t, 1), -jnp.inf, F32)
    zeros = jnp.zeros((t, 1), F32)
    carry = block(i, (neg, neg, zeros, zeros, jnp.zeros((t, LANES), F32)), True)
    carry = lax.fori_loop(0, i, lambda n, c: block(i - 1 - n, c, False), carry)
    denom = jnp.where(first_head, carry[2], carry[3])
    o_ref[0] = (carry[4] / denom).astype(o_ref.dtype)


def _fox_attention(q, kv, c_blocks, batch, seq):
    t = ATTN_TILE
    v_off = D_MODEL // LANES
    return pl.pallas_call(
        _fox_attn_kernel,
        grid=(batch, HEAD_PAIRS, seq // t),
        in_specs=[
            pl.BlockSpec((1, t, LANES), lambda b, p, i: (b, i, p)),
            pl.BlockSpec((1, seq, LANES), lambda b, p, i: (b, 0, p)),
            pl.BlockSpec((1, seq, LANES), lambda b, p, i: (b, 0, v_off + p)),
            pl.BlockSpec((1, 1, seq // t, 2, t), lambda b, p, i: (b, p, 0, 0, 0)),
        ],
        out_specs=pl.BlockSpec((1, t, LANES), lambda b, p, i: (b, i, p)),
        out_shape=jax.ShapeDtypeStruct((batch, seq, D_MODEL), BF16),
        compiler_params=_params("parallel", "parallel", "arbitrary"),
        name="fox_attention",
    )(q, kv, kv, c_blocks)


def _first_index_of_max(x, lane, valid):
    masked = jnp.where(valid, x, -jnp.inf)
    top = jnp.max(masked, axis=1, keepdims=True)
    idx = jnp.min(jnp.where(masked == top, lane, LANES), axis=1, keepdims=True)
    return top, idx


def _router_kernel(h_ref, g_ref, whi_ref, wlo_ref, xn_ref, cmb_ref):
    xn = _rms_normed(h_ref[...], g_ref[...])
    x_hi = xn.astype(BF16)
    xn_ref[...] = x_hi
    x_lo = (xn - x_hi.astype(F32)).astype(BF16)
    logits = (jnp.dot(x_hi, whi_ref[...], preferred_element_type=F32)
              + jnp.dot(x_lo, whi_ref[...], preferred_element_type=F32)
              + jnp.dot(x_hi, wlo_ref[...], preferred_element_type=F32))
    lane = lax.broadcasted_iota(jnp.int32, logits.shape, 1)
    is_group = (lane >= N_EXPERTS) & (lane < N_EXPERTS + N_GROUPS)
    g_top, g_lane = _first_index_of_max(logits, lane, is_group)
    g_sum = jnp.sum(jnp.where(is_group, jnp.exp(logits - g_top), 0.0), axis=1, keepdims=True)
    g_prob = 1.0 / g_sum
    g_idx = g_lane - N_EXPERTS
    in_group = (lane >= g_idx * EXPERTS_PER_GROUP) & (lane < (g_idx + 1) * EXPERTS_PER_GROUP)
    t1, i1 = _first_index_of_max(logits, lane, in_group)
    t2, i2 = _first_index_of_max(logits, lane, in_group & (lane != i1))
    e2 = jnp.exp(t2 - t1)
    w1 = g_prob / (1.0 + e2)
    w2 = g_prob * e2 / (1.0 + e2)
    cmb_ref[...] = jnp.where(lane == i1, w1, 0.0) + jnp.where(lane == i2, w2, 0.0)


def _router(h, gain, w_hi, w_lo):
    n, d = h.shape
    return pl.pallas_call(
        _router_kernel,
        grid=(n // ROW_TILE,),
        in_specs=[
            pl.BlockSpec((ROW_TILE, d), lambda i: (i, 0)),
            pl.BlockSpec((1, d), lambda i: (0, 0)),
            pl.BlockSpec((d, LANES), lambda i: (0, 0)),
            pl.BlockSpec((d, LANES), lambda i: (0, 0)),
        ],
        out_specs=[
            pl.BlockSpec((ROW_TILE, d), lambda i: (i, 0)),
            pl.BlockSpec((ROW_TILE, LANES), lambda i: (i, 0)),
        ],
        out_shape=[
            jax.ShapeDtypeStruct((n, d), BF16),
            jax.ShapeDtypeStruct((n, LANES), F32),
        ],
        compiler_params=_params("parallel"),
        name="moe_router",
    )(h, gain.reshape(1, d), w_hi, w_lo)


def _moe_dense_kernel(xn_ref, cmb_ref, h_ref, wg_ref, wu_ref, wd_ref, o_ref):
    e = pl.program_id(1)

    @pl.when(e == 0)
    def _():
        o_ref[...] = h_ref[...]

    x = xn_ref[...]
    gate = jnp.dot(x, wg_ref[0], preferred_element_type=F32)
    up = jnp.dot(x, wu_ref[0], preferred_element_type=F32)
    cmb = cmb_ref[...]
    lane = lax.broadcasted_iota(jnp.int32, cmb.shape, 1)
    scale = jnp.sum(jnp.where(lane == e, cmb, 0.0), axis=1, keepdims=True)
    act = gate / (1.0 + jnp.exp(-gate)) * up * scale
    o_ref[...] += jnp.dot(act.astype(BF16), wd_ref[0], preferred_element_type=F32)


def _moe_dense(xn, cmb, h, w_gate, w_up, w_down):
    n, d = h.shape
    f = w_gate.shape[2]
    return pl.pallas_call(
        _moe_dense_kernel,
        grid=(n // ROW_TILE, N_EXPERTS),
        in_specs=[
            pl.BlockSpec((ROW_TILE, d), lambda i, e: (i, 0)),
            pl.BlockSpec((ROW_TILE, LANES), lambda i, e: (i, 0)),
            pl.BlockSpec((ROW_TILE, d), lambda i, e: (i, 0)),
            pl.BlockSpec((1, d, f), lambda i, e: (e, 0, 0)),
            pl.BlockSpec((1, d, f), lambda i, e: (e, 0, 0)),
            pl.BlockSpec((1, f, d), lambda i, e: (e, 0, 0)),
        ],
        out_specs=pl.BlockSpec((ROW_TILE, d), lambda i, e: (i, 0)),
        out_shape=jax.ShapeDtypeStruct((n, d), F32),
        compiler_params=_params("parallel", "arbitrary"),
        name="moe_experts",
    )(xn, cmb, h, w_gate, w_up, w_down)


def _rms_norm_kernel(h_ref, g_ref, o_ref):
    o_ref[...] = _rms_normed(h_ref[...], g_ref[...])


def _final_norm(h, gain):
    n, d = h.shape
    return pl.pallas_call(
        _rms_norm_kernel,
        grid=(n // ROW_TILE,),
        in_specs=[
            pl.BlockSpec((ROW_TILE, d), lambda i: (i, 0)),
            pl.BlockSpec((1, d), lambda i: (0, 0)),
        ],
        out_specs=pl.BlockSpec((ROW_TILE, d), lambda i: (i, 0)),
        out_shape=jax.ShapeDtypeStruct((n, d), F32),
        compiler_params=_params("parallel"),
        name="final_norm",
    )(h, gain.reshape(1, d))


def _moe_layer(h, gain, w_group, w_router, w_gate, w_up, w_down):
    d = h.shape[1]
    w_r = jnp.zeros((d, LANES), F32)
    w_r = w_r.at[:, :N_EXPERTS].set(w_router).at[:, N_EXPERTS:N_EXPERTS + N_GROUPS].set(w_group)
    w_hi = w_r.astype(BF16)
    w_lo = (w_r - w_hi.astype(F32)).astype(BF16)
    xn, cmb = _router(h, gain, w_hi, w_lo)
    return _moe_dense(xn, cmb, h, w_gate.astype(BF16), w_up.astype(BF16), w_down.astype(BF16))


def kernel(x, attn_norm, w_qkv_a, w_o_a, kv_norm, w_kvf, b_f, w_q_b, w_o_b, moe_norm,
           w_group, w_router, w_gate, w_up, w_down, final_norm):
    batch, seq, d = x.shape
    n = batch * seq
    q_scale = HEAD_DIM ** -0.5
    h = x.reshape(n, d)
    n_a = w_qkv_a.shape[0]
    depth = attn_norm.shape[0]
    kv = c_blocks = None
    for layer in range(depth):
        if layer < n_a:
            w = w_qkv_a[layer]
            w = jnp.concatenate([w[:, :d] * q_scale, w[:, d:]], axis=1).astype(BF16)
            qkv = _norm_matmul(h, attn_norm[layer], w, 1024)
            mix = _sb_attention(qkv.reshape(batch, seq, 3 * d), batch, seq)
            h = _matmul_residual(mix.reshape(n, d), w_o_a[layer].astype(BF16), h)
        else:
            j = layer - n_a
            if j == 0:
                kv = _norm_matmul(h, kv_norm, w_kvf[:, :2 * d].astype(BF16), 1024)
                w_f_t = w_kvf[:, 2 * d:].T.astype(BF16)
                cum = _forget_gates(h.reshape(batch, seq, d), kv_norm, w_f_t, b_f)
                t = ATTN_TILE
                c_blocks = cum.reshape(batch, HEAD_PAIRS, 2, seq // t, t).transpose(0, 1, 3, 2, 4)
                kv = kv.reshape(batch, seq, 2 * d)
            q = _norm_matmul(h, attn_norm[layer], (w_q_b[j] * q_scale).astype(BF16), 1024)
            mix = _fox_attention(q.reshape(batch, seq, d), kv, c_blocks, batch, seq)
            h = _matmul_residual(mix.reshape(n, d), w_o_b[j].astype(BF16), h)
        h = _moe_layer(h, moe_norm[layer], w_group[layer], w_router[layer],
                       w_gate[layer], w_up[layer], w_down[layer])
    return _final_norm(h, final_norm).reshape(batch, seq, d)
```

```python
import jax
import jax.numpy as jnp
from jax import lax
from jax.experimental import pallas as pl
from jax.experimental.pallas import tpu as pltpu

D_MODEL = 1024
N_HEADS = 16
HEAD_DIM = 64
N_GROUPS = 4
EXPERTS_PER_GROUP = 4
N_EXPERTS = 16
D_EXPERT = 512
RMS_EPS = 1e-6

LANES = 128
HEADS_PER_STEP = 4
HEAD_LANES = HEADS_PER_STEP * HEAD_DIM
HEAD_STEPS = N_HEADS // HEADS_PER_STEP
LOG2_E = 1.4426950408889634
ROW_TILE = 512
ATTN_TILE = 256
VMEM_LIMIT = 48 * 1024 * 1024

F32 = jnp.float32
BF16 = jnp.bfloat16
NT_DIMS = (((1,), (1,)), ((), ()))


def _params(*semantics):
    return pltpu.CompilerParams(dimension_semantics=semantics, vmem_limit_bytes=VMEM_LIMIT)


def _rms_normed(x, gain):
    var = jnp.mean(x * x, axis=-1, keepdims=True)
    return x * lax.rsqrt(var + RMS_EPS) * gain


def _norm_matmul_kernel(h_ref, g_ref, w_ref, o_ref):
    xn = _rms_normed(h_ref[...], g_ref[...]).astype(BF16)
    o_ref[...] = jnp.dot(xn, w_ref[...], preferred_element_type=F32).astype(o_ref.dtype)


def _norm_matmul(h, gain, w, col_tile):
    n, d = h.shape
    n_out = w.shape[1]
    return pl.pallas_call(
        _norm_matmul_kernel,
        grid=(n_out // col_tile, n // ROW_TILE),
        in_specs=[
            pl.BlockSpec((ROW_TILE, d), lambda j, i: (i, 0)),
            pl.BlockSpec((1, d), lambda j, i: (0, 0)),
            pl.BlockSpec((d, col_tile), lambda j, i: (0, j)),
        ],
        out_specs=pl.BlockSpec((ROW_TILE, col_tile), lambda j, i: (i, j)),
        out_shape=jax.ShapeDtypeStruct((n, n_out), BF16),
        compiler_params=_params("parallel", "arbitrary"),
        name="norm_matmul",
    )(h, gain.reshape(1, d), w)


def _matmul_residual_kernel(mix_ref, w_ref, h_ref, o_ref):
    o_ref[...] = h_ref[...] + jnp.dot(mix_ref[...], w_ref[...], preferred_element_type=F32)


def _matmul_residual(mix, w, h):
    n, d = h.shape
    return pl.pallas_call(
        _matmul_residual_kernel,
        grid=(n // ROW_TILE,),
        in_specs=[
            pl.BlockSpec((ROW_TILE, d), lambda i: (i, 0)),
            pl.BlockSpec((d, d), lambda i: (0, 0)),
            pl.BlockSpec((ROW_TILE, d), lambda i: (i, 0)),
        ],
        out_specs=pl.BlockSpec((ROW_TILE, d), lambda i: (i, 0)),
        out_shape=jax.ShapeDtypeStruct((n, d), F32),
        compiler_params=_params("parallel"),
        name="matmul_residual",
    )(mix, w, h)


def _head_ids(rows):
    lane = lax.broadcasted_iota(jnp.int32, (rows, HEAD_LANES), 1)
    return lane // HEAD_DIM


def _only_head(x, head_ids, h):
    return jnp.where(head_ids == h, x, jnp.zeros_like(x))


def _per_head_lanes(head_ids, cols):
    out = cols[-1]
    for h in range(HEADS_PER_STEP - 2, -1, -1):
        out = jnp.where(head_ids == h, cols[h], out)
    return out


def _sb_attn_kernel(q_ref, k_ref, v_ref, o_ref):
    t = ATTN_TILE
    i = pl.program_id(2)
    head_ids = _head_ids(t)
    q = q_ref[0]
    q_heads = [_only_head(q, head_ids, h) for h in range(HEADS_PER_STEP)]
    row = lax.broadcasted_iota(jnp.int32, (t, t), 0)
    col = lax.broadcasted_iota(jnp.int32, (t, t), 1)
    strict = col < row
    neg_suffix_ones = jnp.where(row >= col, -1.0, 0.0).astype(BF16)
    sign_bit = jnp.uint32(0x80000000)

    def block(j, carry, diagonal):
        cs, acc = carry
        start = pl.multiple_of(j * t, t)
        kb = k_ref[0, pl.ds(start, t), :]
        vb = v_ref[0, pl.ds(start, t), :]
        new_cs = []
        for h in range(HEADS_PER_STEP):
            z = lax.dot_general(q_heads[h], kb, NT_DIMS, preferred_element_type=F32)
            neg_abs = lax.bitcast_convert_type(lax.bitcast_convert_type(z, jnp.uint32) | sign_bit, F32)
            sp = jnp.maximum(z, 0.0) + jnp.log(1.0 + jnp.exp2(neg_abs)) * LOG2_E
            if diagonal:
                sp = jnp.where(strict, sp, 0.0)
            suffix = jnp.dot(sp.astype(BF16), neg_suffix_ones, preferred_element_type=F32)
            w = jnp.exp2(z + suffix + cs[h])
            if diagonal:
                w = jnp.where(strict, w, 0.0)
            acc = acc + jnp.dot(w.astype(BF16), _only_head(vb, head_ids, h), preferred_element_type=F32)
            new_cs.append(cs[h] - jnp.sum(sp, axis=1, keepdims=True))
        return tuple(new_cs), acc

    zeros = jnp.zeros((t, 1), F32)
    carry = block(i, ((zeros,) * HEADS_PER_STEP, jnp.zeros((t, HEAD_LANES), F32)), True)
    carry = lax.fori_loop(0, i, lambda n, c: block(i - 1 - n, c, False), carry)
    o_ref[0] = carry[1].astype(o_ref.dtype)


def _sb_attention(qkv, batch, seq):
    t = ATTN_TILE
    k_off = D_MODEL // HEAD_LANES
    return pl.pallas_call(
        _sb_attn_kernel,
        grid=(batch, HEAD_STEPS, seq // t),
        in_specs=[
            pl.BlockSpec((1, t, HEAD_LANES), lambda b, p, i: (b, i, p)),
            pl.BlockSpec((1, seq, HEAD_LANES), lambda b, p, i: (b, 0, k_off + p)),
            pl.BlockSpec((1, seq, HEAD_LANES), lambda b, p, i: (b, 0, 2 * k_off + p)),
        ],
        out_specs=pl.BlockSpec((1, t, HEAD_LANES), lambda b, p, i: (b, i, p)),
        out_shape=jax.ShapeDtypeStruct((batch, seq, D_MODEL), BF16),
        compiler_params=_params("parallel", "parallel", "arbitrary"),
        name="sb_attention",
    )(qkv, qkv, qkv)


def _forget_gate_kernel(h_ref, g_ref, wf_ref, bf_ref, o_ref):
    t = ATTN_TILE
    seq = h_ref.shape[1]
    u = _rms_normed(h_ref[0], g_ref[...]).astype(BF16)
    logits = lax.dot_general(wf_ref[...], u, NT_DIMS, preferred_element_type=F32) + bf_ref[...]
    log_f = jnp.minimum(logits, 0.0) - jnp.log(1.0 + jnp.exp(-jnp.abs(logits)))
    row = lax.broadcasted_iota(jnp.int32, (t, t), 0)
    col = lax.broadcasted_iota(jnp.int32, (t, t), 1)
    prefix_ones = (row <= col).astype(BF16)
    carry = jnp.zeros((N_HEADS, 1), F32)
    for n in range(seq // t):
        blk = log_f[:, n * t:(n + 1) * t]
        hi = blk.astype(BF16)
        lo = (blk - hi.astype(F32)).astype(BF16)
        cs = (jnp.dot(hi, prefix_ones, preferred_element_type=F32)
              + jnp.dot(lo, prefix_ones, preferred_element_type=F32) + carry)
        o_ref[0, :, n * t:(n + 1) * t] = cs * LOG2_E
        carry = cs[:, t - 1:t]


def _forget_gates(h3, gain, w_f_t, b_f):
    batch, seq, d = h3.shape
    return pl.pallas_call(
        _forget_gate_kernel,
        grid=(batch,),
        in_specs=[
            pl.BlockSpec((1, seq, d), lambda b: (b, 0, 0)),
            pl.BlockSpec((1, d), lambda b: (0, 0)),
            pl.BlockSpec((N_HEADS, d), lambda b: (0, 0)),
            pl.BlockSpec((N_HEADS, 1), lambda b: (0, 0)),
        ],
        out_specs=pl.BlockSpec((1, N_HEADS, seq), lambda b: (b, 0, 0)),
        out_shape=jax.ShapeDtypeStruct((batch, N_HEADS, seq), F32),
        compiler_params=_params("parallel"),
        name="forget_gates",
    )(h3, gain.reshape(1, d), w_f_t, b_f.reshape(N_HEADS, 1))


def _fox_attn_kernel(q_ref, k_ref, v_ref, c_ref, o_ref):
    t = ATTN_TILE
    i = pl.program_id(2)
    head_ids = _head_ids(t)
    q = q_ref[0]
    q_heads = [_only_head(q, head_ids, h) for h in range(HEADS_PER_STEP)]
    row = lax.broadcasted_iota(jnp.int32, (t, t), 0)
    col = lax.broadcasted_iota(jnp.int32, (t, t), 1)
    causal = col <= row

    def block(j, carry, diagonal):
        ms, ls, acc = carry
        start = pl.multiple_of(j * t, t)
        kb = k_ref[0, pl.ds(start, t), :]
        vb = v_ref[0, pl.ds(start, t), :]
        cb = c_ref[0, 0, j]
        new_ms, new_ls, alphas = [], [], []
        pv = jnp.zeros((t, HEAD_LANES), F32)
        for h in range(HEADS_PER_STEP):
            s = lax.dot_general(q_heads[h], kb, NT_DIMS, preferred_element_type=F32) - cb[h:h + 1, :]
            if diagonal:
                s = jnp.where(causal, s, -jnp.inf)
            m_new = jnp.maximum(ms[h], jnp.max(s, axis=1, keepdims=True))
            p = jnp.exp2(s - m_new)
            alpha = jnp.exp2(ms[h] - m_new)
            new_ms.append(m_new)
            new_ls.append(alpha * ls[h] + jnp.sum(p, axis=1, keepdims=True))
            alphas.append(alpha)
            pv = pv + jnp.dot(p.astype(BF16), _only_head(vb, head_ids, h), preferred_element_type=F32)
        acc = acc * _per_head_lanes(head_ids, alphas) + pv
        return tuple(new_ms), tuple(new_ls), acc

    neg = jnp.full((t, 1), -jnp.inf, F32)
    zeros = jnp.zeros((t, 1), F32)
    init = ((neg,) * HEADS_PER_STEP, (zeros,) * HEADS_PER_STEP, jnp.zeros((t, HEAD_LANES), F32))
    carry = block(i, init, True)
    carry = lax.fori_loop(0, i, lambda n, c: block(i - 1 - n, c, False), carry)
    o_ref[0] = (carry[2] / _per_head_lanes(head_ids, carry[1])).astype(o_ref.dtype)


def _fox_attention(q, kv, c_blocks, batch, seq):
    t = ATTN_TILE
    v_off = D_MODEL // HEAD_LANES
    return pl.pallas_call(
        _fox_attn_kernel,
        grid=(batch, HEAD_STEPS, seq // t),
        in_specs=[
            pl.BlockSpec((1, t, HEAD_LANES), lambda b, p, i: (b, i, p)),
            pl.BlockSpec((1, seq, HEAD_LANES), lambda b, p, i: (b, 0, p)),
            pl.BlockSpec((1, seq, HEAD_LANES), lambda b, p, i: (b, 0, v_off + p)),
            pl.BlockSpec((1, 1, seq // t, HEADS_PER_STEP, t), lambda b, p, i: (b, p, 0, 0, 0)),
        ],
        out_specs=pl.BlockSpec((1, t, HEAD_LANES), lambda b, p, i: (b, i, p)),
        out_shape=jax.ShapeDtypeStruct((batch, seq, D_MODEL), BF16),
        compiler_params=_params("parallel", "parallel", "arbitrary"),
        name="fox_attention",
    )(q, kv, kv, c_blocks)


def _first_index_of_max(x, lane, valid):
    masked = jnp.where(valid, x, -jnp.inf)
    top = jnp.max(masked, axis=1, keepdims=True)
    idx = jnp.min(jnp.where(masked == top, lane, LANES), axis=1, keepdims=True)
    return top, idx


def _router_kernel(h_ref, g_ref, whi_ref, wlo_ref, xn_ref, cmb_ref):
    xn = _rms_normed(h_ref[...], g_ref[...])
    x_hi = xn.astype(BF16)
    xn_ref[...] = x_hi
    x_lo = (xn - x_hi.astype(F32)).astype(BF16)
    logits = (jnp.dot(x_hi, whi_ref[...], preferred_element_type=F32)
              + jnp.dot(x_lo, whi_ref[...], preferred_element_type=F32)
              + jnp.dot(x_hi, wlo_ref[...], preferred_element_type=F32))
    lane = lax.broadcasted_iota(jnp.int32, logits.shape, 1)
    is_group = (lane >= N_EXPERTS) & (lane < N_EXPERTS + N_GROUPS)
    g_top, g_lane = _first_index_of_max(logits, lane, is_group)
    g_sum = jnp.sum(jnp.where(is_group, jnp.exp(logits - g_top), 0.0), axis=1, keepdims=True)
    g_prob = 1.0 / g_sum
    g_idx = g_lane - N_EXPERTS
    in_group = (lane >= g_idx * EXPERTS_PER_GROUP) & (lane < (g_idx + 1) * EXPERTS_PER_GROUP)
    t1, i1 = _first_index_of_max(logits, lane, in_group)
    t2, i2 = _first_index_of_max(logits, lane, in_group & (lane != i1))
    e2 = jnp.exp(t2 - t1)
    w1 = g_prob / (1.0 + e2)
    w2 = g_prob * e2 / (1.0 + e2)
    cmb_ref[...] = jnp.where(lane == i1, w1, 0.0) + jnp.where(lane == i2, w2, 0.0)


def _router(h, gain, w_hi, w_lo):
    n, d = h.shape
    return pl.pallas_call(
        _router_kernel,
        grid=(n // ROW_TILE,),
        in_specs=[
            pl.BlockSpec((ROW_TILE, d), lambda i: (i, 0)),
            pl.BlockSpec((1, d), lambda i: (0, 0)),
            pl.BlockSpec((d, LANES), lambda i: (0, 0)),
            pl.BlockSpec((d, LANES), lambda i: (0, 0)),
        ],
        out_specs=[
            pl.BlockSpec((ROW_TILE, d), lambda i: (i, 0)),
            pl.BlockSpec((ROW_TILE, LANES), lambda i: (i, 0)),
        ],
        out_shape=[
            jax.ShapeDtypeStruct((n, d), BF16),
            jax.ShapeDtypeStruct((n, LANES), F32),
        ],
        compiler_params=_params("parallel"),
        name="moe_router",
    )(h, gain.reshape(1, d), w_hi, w_lo)


def _moe_dense_kernel(xn_ref, cmb_ref, h_ref, wg_ref, wu_ref, wd_ref, o_ref):
    e = pl.program_id(1)

    @pl.when(e == 0)
    def _():
        o_ref[...] = h_ref[...]

    x = xn_ref[...]
    gate = jnp.dot(x, wg_ref[0], preferred_element_type=F32)
    up = jnp.dot(x, wu_ref[0], preferred_element_type=F32)
    cmb = cmb_ref[...]
    lane = lax.broadcasted_iota(jnp.int32, cmb.shape, 1)
    scale = jnp.sum(jnp.where(lane == e, cmb, 0.0), axis=1, keepdims=True)
    act = gate / (1.0 + jnp.exp(-gate)) * up * scale
    o_ref[...] += jnp.dot(act.astype(BF16), wd_ref[0], preferred_element_type=F32)


def _moe_dense(xn, cmb, h, w_gate, w_up, w_down):
    n, d = h.shape
    f = w_gate.shape[2]
    return pl.pallas_call(
        _moe_dense_kernel,
        grid=(n // ROW_TILE, N_EXPERTS),
        in_specs=[
            pl.BlockSpec((ROW_TILE, d), lambda i, e: (i, 0)),
            pl.BlockSpec((ROW_TILE, LANES), lambda i, e: (i, 0)),
            pl.BlockSpec((ROW_TILE, d), lambda i, e: (i, 0)),
            pl.BlockSpec((1, d, f), lambda i, e: (e, 0, 0)),
            pl.BlockSpec((1, d, f), lambda i, e: (e, 0, 0)),
            pl.BlockSpec((1, f, d), lambda i, e: (e, 0, 0)),
        ],
        out_specs=pl.BlockSpec((ROW_TILE, d), lambda i, e: (i, 0)),
        out_shape=jax.ShapeDtypeStruct((n, d), F32),
        compiler_params=_params("parallel", "arbitrary"),
        name="moe_experts",
    )(xn, cmb, h, w_gate, w_up, w_down)


def _rms_norm_kernel(h_ref, g_ref, o_ref):
    o_ref[...] = _rms_normed(h_ref[...], g_ref[...])


def _final_norm(h, gain):
    n, d = h.shape
    return pl.pallas_call(
        _rms_norm_kernel,
        grid=(n // ROW_TILE,),
        in_specs=[
            pl.BlockSpec((ROW_TILE, d), lambda i: (i, 0)),
            pl.BlockSpec((1, d), lambda i: (0, 0)),
        ],
        out_specs=pl.BlockSpec((ROW_TILE, d), lambda i: (i, 0)),
        out_shape=jax.ShapeDtypeStruct((n, d), F32),
        compiler_params=_params("parallel"),
        name="final_norm",
    )(h, gain.reshape(1, d))


def _moe_layer(h, gain, w_group, w_router, w_gate, w_up, w_down):
    d = h.shape[1]
    w_r = jnp.zeros((d, LANES), F32)
    w_r = w_r.at[:, :N_EXPERTS].set(w_router).at[:, N_EXPERTS:N_EXPERTS + N_GROUPS].set(w_group)
    w_hi = w_r.astype(BF16)
    w_lo = (w_r - w_hi.astype(F32)).astype(BF16)
    xn, cmb = _router(h, gain, w_hi, w_lo)
    return _moe_dense(xn, cmb, h, w_gate.astype(BF16), w_up.astype(BF16), w_down.astype(BF16))


def kernel(x, attn_norm, w_qkv_a, w_o_a, kv_norm, w_kvf, b_f, w_q_b, w_o_b, moe_norm,
           w_group, w_router, w_gate, w_up, w_down, final_norm):
    batch, seq, d = x.shape
    n = batch * seq
    q_scale = HEAD_DIM ** -0.5 * LOG2_E
    h = x.reshape(n, d)
    n_a = w_qkv_a.shape[0]
    depth = attn_norm.shape[0]
    kv = c_blocks = None
    for layer in range(depth):
        if layer < n_a:
            w = w_qkv_a[layer]
            w = jnp.concatenate([w[:, :d] * q_scale, w[:, d:]], axis=1).astype(BF16)
            qkv = _norm_matmul(h, attn_norm[layer], w, 1024)
            mix = _sb_attention(qkv.reshape(batch, seq, 3 * d), batch, seq)
            h = _matmul_residual(mix.reshape(n, d), w_o_a[layer].astype(BF16), h)
        else:
            j = layer - n_a
            if j == 0:
                kv = _norm_matmul(h, kv_norm, w_kvf[:, :2 * d].astype(BF16), 1024)
                w_f_t = w_kvf[:, 2 * d:].T.astype(BF16)
                cum = _forget_gates(h.reshape(batch, seq, d), kv_norm, w_f_t, b_f)
                t = ATTN_TILE
                c_blocks = cum.reshape(batch, HEAD_STEPS, HEADS_PER_STEP, seq // t, t).transpose(0, 1, 3, 2, 4)
                kv = kv.reshape(batch, seq, 2 * d)
            q = _norm_matmul(h, attn_norm[layer], (w_q_b[j] * q_scale).astype(BF16), 1024)
            mix = _fox_attention(q.reshape(batch, seq, d), kv, c_blocks, batch, seq)
            h = _matmul_residual(mix.reshape(n, d), w_o_b[j].astype(BF16), h)
        h = _moe_layer(h, moe_norm[layer], w_group[layer], w_router[layer],
                       w_gate[layer], w_up[layer], w_down[layer])
    return _final_norm(h, final_norm).reshape(batch, seq, d)
```

```python
import jax
import jax.numpy as jnp
from jax import lax
from jax.experimental import pallas as pl
from jax.experimental.pallas import tpu as pltpu

D_MODEL = 1024
N_HEADS = 16
HEAD_DIM = 64
N_GROUPS = 4
EXPERTS_PER_GROUP = 4
N_EXPERTS = 16
D_EXPERT = 512
RMS_EPS = 1e-6

LANES = 128
HEADS_PER_STEP = 4
HEAD_LANES = HEADS_PER_STEP * HEAD_DIM
HEAD_STEPS = N_HEADS // HEADS_PER_STEP
LOG2_E = 1.4426950408889634
ROW_TILE = 512
ATTN_TILE = 256
VMEM_LIMIT = 48 * 1024 * 1024

F32 = jnp.float32
BF16 = jnp.bfloat16
NT_DIMS = (((1,), (1,)), ((), ()))


def _params(*semantics):
    return pltpu.CompilerParams(dimension_semantics=semantics, vmem_limit_bytes=VMEM_LIMIT)


def _rms_normed(x, gain):
    var = jnp.mean(x * x, axis=-1, keepdims=True)
    return x * lax.rsqrt(var + RMS_EPS) * gain


def _norm_matmul_kernel(h_ref, g_ref, w_ref, o_ref):
    xn = _rms_normed(h_ref[...], g_ref[...]).astype(BF16)
    o_ref[...] = jnp.dot(xn, w_ref[...], preferred_element_type=F32).astype(o_ref.dtype)


def _norm_matmul(h, gain, w, col_tile):
    n, d = h.shape
    n_out = w.shape[1]
    return pl.pallas_call(
        _norm_matmul_kernel,
        grid=(n_out // col_tile, n // ROW_TILE),
        in_specs=[
            pl.BlockSpec((ROW_TILE, d), lambda j, i: (i, 0)),
            pl.BlockSpec((1, d), lambda j, i: (0, 0)),
            pl.BlockSpec((d, col_tile), lambda j, i: (0, j)),
        ],
        out_specs=pl.BlockSpec((ROW_TILE, col_tile), lambda j, i: (i, j)),
        out_shape=jax.ShapeDtypeStruct((n, n_out), BF16),
        compiler_params=_params("parallel", "arbitrary"),
        name="norm_matmul",
    )(h, gain.reshape(1, d), w)


def _matmul_residual_kernel(mix_ref, w_ref, h_ref, o_ref):
    o_ref[...] = h_ref[...] + jnp.dot(mix_ref[...], w_ref[...], preferred_element_type=F32)


def _matmul_residual(mix, w, h):
    n, d = h.shape
    return pl.pallas_call(
        _matmul_residual_kernel,
        grid=(n // ROW_TILE,),
        in_specs=[
            pl.BlockSpec((ROW_TILE, d), lambda i: (i, 0)),
            pl.BlockSpec((d, d), lambda i: (0, 0)),
            pl.BlockSpec((ROW_TILE, d), lambda i: (i, 0)),
        ],
        out_specs=pl.BlockSpec((ROW_TILE, d), lambda i: (i, 0)),
        out_shape=jax.ShapeDtypeStruct((n, d), F32),
        compiler_params=_params("parallel"),
        name="matmul_residual",
    )(mix, w, h)


def _head_ids(rows):
    lane = lax.broadcasted_iota(jnp.int32, (rows, HEAD_LANES), 1)
    return lane // HEAD_DIM


def _only_head(x, head_ids, h):
    return jnp.where(head_ids == h, x, jnp.zeros_like(x))


def _per_head_lanes(head_ids, cols):
    out = cols[-1]
    for h in range(HEADS_PER_STEP - 2, -1, -1):
        out = jnp.where(head_ids == h, cols[h], out)
    return out


def _sb_attn_kernel(q_ref, k_ref, v_ref, o_ref):
    t = ATTN_TILE
    i = pl.program_id(2)
    head_ids = _head_ids(t)
    q = q_ref[0]
    q_heads = [_only_head(q, head_ids, h) for h in range(HEADS_PER_STEP)]
    row = lax.broadcasted_iota(jnp.int32, (t, t), 0)
    col = lax.broadcasted_iota(jnp.int32, (t, t), 1)
    strict = col < row
    neg_suffix_ones = jnp.where(row >= col, -1.0, 0.0).astype(BF16)

    def scores(j):
        kb = k_ref[0, pl.ds(pl.multiple_of(j * t, t), t), :]
        return tuple(lax.dot_general(qh, kb, NT_DIMS, preferred_element_type=F32) for qh in q_heads)

    def weights(zs, cs, diagonal):
        ws, new_cs = [], []
        for z, c in zip(zs, cs):
            sp = jnp.maximum(z, 0.0) + jnp.log(1.0 + jnp.exp2(-jnp.abs(z))) * LOG2_E
            if diagonal:
                sp = jnp.where(strict, sp, 0.0)
            suffix = jnp.dot(sp.astype(BF16), neg_suffix_ones, preferred_element_type=F32)
            w = jnp.exp2(z + suffix + c)
            if diagonal:
                w = jnp.where(strict, w, 0.0)
            ws.append(w.astype(BF16))
            new_cs.append(c - jnp.sum(sp, axis=1, keepdims=True))
        return tuple(ws), tuple(new_cs)

    def add_pv(acc, ws, j):
        vb = v_ref[0, pl.ds(pl.multiple_of(j * t, t), t), :]
        for h, w in enumerate(ws):
            acc = acc + jnp.dot(w, _only_head(vb, head_ids, h), preferred_element_type=F32)
        return acc

    def trip(n, carry):
        ws_prev, cs, acc = carry
        j = i - 1 - n
        acc = add_pv(acc, ws_prev, j + 1)
        ws, cs = weights(scores(j), cs, False)
        return ws, cs, acc

    zeros = jnp.zeros((t, 1), F32)
    ws, cs = weights(scores(i), (zeros,) * HEADS_PER_STEP, True)
    ws, _, acc = lax.fori_loop(0, i, trip, (ws, cs, jnp.zeros((t, HEAD_LANES), F32)))
    o_ref[0] = add_pv(acc, ws, 0).astype(o_ref.dtype)


def _sb_attention(qkv, batch, seq):
    t = ATTN_TILE
    k_off = D_MODEL // HEAD_LANES
    return pl.pallas_call(
        _sb_attn_kernel,
        grid=(batch, HEAD_STEPS, seq // t),
        in_specs=[
            pl.BlockSpec((1, t, HEAD_LANES), lambda b, p, i: (b, i, p)),
            pl.BlockSpec((1, seq, HEAD_LANES), lambda b, p, i: (b, 0, k_off + p)),
            pl.BlockSpec((1, seq, HEAD_LANES), lambda b, p, i: (b, 0, 2 * k_off + p)),
        ],
        out_specs=pl.BlockSpec((1, t, HEAD_LANES), lambda b, p, i: (b, i, p)),
        out_shape=jax.ShapeDtypeStruct((batch, seq, D_MODEL), BF16),
        compiler_params=_params("parallel", "parallel", "arbitrary"),
        name="sb_attention",
    )(qkv, qkv, qkv)


def _forget_gate_kernel(h_ref, g_ref, wf_ref, bf_ref, o_ref):
    t = ATTN_TILE
    seq = h_ref.shape[1]
    u = _rms_normed(h_ref[0], g_ref[...]).astype(BF16)
    logits = lax.dot_general(wf_ref[...], u, NT_DIMS, preferred_element_type=F32) + bf_ref[...]
    log_f = jnp.minimum(logits, 0.0) - jnp.log(1.0 + jnp.exp(-jnp.abs(logits)))
    row = lax.broadcasted_iota(jnp.int32, (t, t), 0)
    col = lax.broadcasted_iota(jnp.int32, (t, t), 1)
    prefix_ones = (row <= col).astype(BF16)
    carry = jnp.zeros((N_HEADS, 1), F32)
    for n in range(seq // t):
        blk = log_f[:, n * t:(n + 1) * t]
        hi = blk.astype(BF16)
        lo = (blk - hi.astype(F32)).astype(BF16)
        cs = (jnp.dot(hi, prefix_ones, preferred_element_type=F32)
              + jnp.dot(lo, prefix_ones, preferred_element_type=F32) + carry)
        o_ref[0, :, n * t:(n + 1) * t] = cs * LOG2_E
        carry = cs[:, t - 1:t]


def _forget_gates(h3, gain, w_f_t, b_f):
    batch, seq, d = h3.shape
    return pl.pallas_call(
        _forget_gate_kernel,
        grid=(batch,),
        in_specs=[
            pl.BlockSpec((1, seq, d), lambda b: (b, 0, 0)),
            pl.BlockSpec((1, d), lambda b: (0, 0)),
            pl.BlockSpec((N_HEADS, d), lambda b: (0, 0)),
            pl.BlockSpec((N_HEADS, 1), lambda b: (0, 0)),
        ],
        out_specs=pl.BlockSpec((1, N_HEADS, seq), lambda b: (b, 0, 0)),
        out_shape=jax.ShapeDtypeStruct((batch, N_HEADS, seq), F32),
        compiler_params=_params("parallel"),
        name="forget_gates",
    )(h3, gain.reshape(1, d), w_f_t, b_f.reshape(N_HEADS, 1))


def _fox_attn_kernel(q_ref, k_ref, v_ref, c_ref, o_ref):
    t = ATTN_TILE
    i = pl.program_id(2)
    head_ids = _head_ids(t)
    q = q_ref[0]
    q_heads = [_only_head(q, head_ids, h) for h in range(HEADS_PER_STEP)]
    row = lax.broadcasted_iota(jnp.int32, (t, t), 0)
    col = lax.broadcasted_iota(jnp.int32, (t, t), 1)
    causal = col <= row

    def scores(j):
        kb = k_ref[0, pl.ds(pl.multiple_of(j * t, t), t), :]
        cb = c_ref[0, 0, j]
        return tuple(lax.dot_general(qh, kb, NT_DIMS, preferred_element_type=F32) - cb[h:h + 1, :]
                     for h, qh in enumerate(q_heads))

    def probs(ss, ms, ls, diagonal):
        ps, new_ms, new_ls, alphas = [], [], [], []
        for s, m, l in zip(ss, ms, ls):
            if diagonal:
                s = jnp.where(causal, s, -jnp.inf)
            m_new = jnp.maximum(m, jnp.max(s, axis=1, keepdims=True))
            p = jnp.exp2(s - m_new)
            alpha = jnp.exp2(m - m_new)
            ps.append(p.astype(BF16))
            new_ms.append(m_new)
            new_ls.append(alpha * l + jnp.sum(p, axis=1, keepdims=True))
            alphas.append(alpha)
        return tuple(ps), tuple(new_ms), tuple(new_ls), alphas

    def add_pv(acc, ps, j):
        vb = v_ref[0, pl.ds(pl.multiple_of(j * t, t), t), :]
        for h, p in enumerate(ps):
            acc = acc + jnp.dot(p, _only_head(vb, head_ids, h), preferred_element_type=F32)
        return acc

    def trip(n, carry):
        ps_prev, ms, ls, acc = carry
        j = i - 1 - n
        acc = add_pv(acc, ps_prev, j + 1)
        ps, ms, ls, alphas = probs(scores(j), ms, ls, False)
        return ps, ms, ls, acc * _per_head_lanes(head_ids, alphas)

    neg = jnp.full((t, 1), -jnp.inf, F32)
    zeros = jnp.zeros((t, 1), F32)
    ps, ms, ls, _ = probs(scores(i), (neg,) * HEADS_PER_STEP, (zeros,) * HEADS_PER_STEP, True)
    carry = (ps, ms, ls, jnp.zeros((t, HEAD_LANES), F32))
    ps, _, ls, acc = lax.fori_loop(0, i, trip, carry)
    o_ref[0] = (add_pv(acc, ps, 0) / _per_head_lanes(head_ids, ls)).astype(o_ref.dtype)


def _fox_attention(q, kv, c_blocks, batch, seq):
    t = ATTN_TILE
    v_off = D_MODEL // HEAD_LANES
    return pl.pallas_call(
        _fox_attn_kernel,
        grid=(batch, HEAD_STEPS, seq // t),
        in_specs=[
            pl.BlockSpec((1, t, HEAD_LANES), lambda b, p, i: (b, i, p)),
            pl.BlockSpec((1, seq, HEAD_LANES), lambda b, p, i: (b, 0, p)),
            pl.BlockSpec((1, seq, HEAD_LANES), lambda b, p, i: (b, 0, v_off + p)),
            pl.BlockSpec((1, 1, seq // t, HEADS_PER_STEP, t), lambda b, p, i: (b, p, 0, 0, 0)),
        ],
        out_specs=pl.BlockSpec((1, t, HEAD_LANES), lambda b, p, i: (b, i, p)),
        out_shape=jax.ShapeDtypeStruct((batch, seq, D_MODEL), BF16),
        compiler_params=_params("parallel", "parallel", "arbitrary"),
        name="fox_attention",
    )(q, kv, kv, c_blocks)


def _first_index_of_max(x, lane, valid):
    masked = jnp.where(valid, x, -jnp.inf)
    top = jnp.max(masked, axis=1, keepdims=True)
    idx = jnp.min(jnp.where(masked == top, lane, LANES), axis=1, keepdims=True)
    return top, idx


def _router_kernel(h_ref, g_ref, whi_ref, wlo_ref, xn_ref, cmb_ref):
    xn = _rms_normed(h_ref[...], g_ref[...])
    x_hi = xn.astype(BF16)
    xn_ref[...] = x_hi
    x_lo = (xn - x_hi.astype(F32)).astype(BF16)
    logits = (jnp.dot(x_hi, whi_ref[...], preferred_element_type=F32)
              + jnp.dot(x_lo, whi_ref[...], preferred_element_type=F32)
              + jnp.dot(x_hi, wlo_ref[...], preferred_element_type=F32))
    lane = lax.broadcasted_iota(jnp.int32, logits.shape, 1)
    is_group = (lane >= N_EXPERTS) & (lane < N_EXPERTS + N_GROUPS)
    g_top, g_lane = _first_index_of_max(logits, lane, is_group)
    g_sum = jnp.sum(jnp.where(is_group, jnp.exp(logits - g_top), 0.0), axis=1, keepdims=True)
    g_prob = 1.0 / g_sum
    g_idx = g_lane - N_EXPERTS
    in_group = (lane >= g_idx * EXPERTS_PER_GROUP) & (lane < (g_idx + 1) * EXPERTS_PER_GROUP)
    t1, i1 = _first_index_of_max(logits, lane, in_group)
    t2, i2 = _first_index_of_max(logits, lane, in_group & (lane != i1))
    e2 = jnp.exp(t2 - t1)
    w1 = g_prob / (1.0 + e2)
    w2 = g_prob * e2 / (1.0 + e2)
    cmb_ref[...] = jnp.where(lane == i1, w1, 0.0) + jnp.where(lane == i2, w2, 0.0)


def _router(h, gain, w_hi, w_lo):
    n, d = h.shape
    return pl.pallas_call(
        _router_kernel,
        grid=(n // ROW_TILE,),
        in_specs=[
            pl.BlockSpec((ROW_TILE, d), lambda i: (i, 0)),
            pl.BlockSpec((1, d), lambda i: (0, 0)),
            pl.BlockSpec((d, LANES), lambda i: (0, 0)),
            pl.BlockSpec((d, LANES), lambda i: (0, 0)),
        ],
        out_specs=[
            pl.BlockSpec((ROW_TILE, d), lambda i: (i, 0)),
            pl.BlockSpec((ROW_TILE, LANES), lambda i: (i, 0)),
        ],
        out_shape=[
            jax.ShapeDtypeStruct((n, d), BF16),
            jax.ShapeDtypeStruct((n, LANES), F32),
        ],
        compiler_params=_params("parallel"),
        name="moe_router",
    )(h, gain.reshape(1, d), w_hi, w_lo)


def _moe_dense_kernel(xn_ref, cmb_ref, h_ref, wg_ref, wu_ref, wd_ref, o_ref):
    e = pl.program_id(1)

    @pl.when(e == 0)
    def _():
        o_ref[...] = h_ref[...]

    x = xn_ref[...]
    gate = jnp.dot(x, wg_ref[0], preferred_element_type=F32)
    up = jnp.dot(x, wu_ref[0], preferred_element_type=F32)
    cmb = cmb_ref[...]
    lane = lax.broadcasted_iota(jnp.int32, cmb.shape, 1)
    scale = jnp.sum(jnp.where(lane == e, cmb, 0.0), axis=1, keepdims=True)
    act = gate / (1.0 + jnp.exp(-gate)) * up * scale
    o_ref[...] += jnp.dot(act.astype(BF16), wd_ref[0], preferred_element_type=F32)


def _moe_dense(xn, cmb, h, w_gate, w_up, w_down):
    n, d = h.shape
    f = w_gate.shape[2]
    return pl.pallas_call(
        _moe_dense_kernel,
        grid=(n // ROW_TILE, N_EXPERTS),
        in_specs=[
            pl.BlockSpec((ROW_TILE, d), lambda i, e: (i, 0)),
            pl.BlockSpec((ROW_TILE, LANES), lambda i, e: (i, 0)),
            pl.BlockSpec((ROW_TILE, d), lambda i, e: (i, 0)),
            pl.BlockSpec((1, d, f), lambda i, e: (e, 0, 0)),
            pl.BlockSpec((1, d, f), lambda i, e: (e, 0, 0)),
            pl.BlockSpec((1, f, d), lambda i, e: (e, 0, 0)),
        ],
        out_specs=pl.BlockSpec((ROW_TILE, d), lambda i, e: (i, 0)),
        out_shape=jax.ShapeDtypeStruct((n, d), F32),
        compiler_params=_params("parallel", "arbitrary"),
        name="moe_experts",
    )(xn, cmb, h, w_gate, w_up, w_down)


def _rms_norm_kernel(h_ref, g_ref, o_ref):
    o_ref[...] = _rms_normed(h_ref[...], g_ref[...])


def _final_norm(h, gain):
    n, d = h.shape
    return pl.pallas_call(
        _rms_norm_kernel,
        grid=(n // ROW_TILE,),
        in_specs=[
            pl.BlockSpec((ROW_TILE, d), lambda i: (i, 0)),
            pl.BlockSpec((1, d), lambda i: (0, 0)),
        ],
        out_specs=pl.BlockSpec((ROW_TILE, d), lambda i: (i, 0)),
        out_shape=jax.ShapeDtypeStruct((n, d), F32),
        compiler_params=_params("parallel"),
        name="final_norm",
    )(h, gain.reshape(1, d))


def _moe_layer(h, gain, w_group, w_router, w_gate, w_up, w_down):
    d = h.shape[1]
    w_r = jnp.zeros((d, LANES), F32)
    w_r = w_r.at[:, :N_EXPERTS].set(w_router).at[:, N_EXPERTS:N_EXPERTS + N_GROUPS].set(w_group)
    w_hi = w_r.astype(BF16)
    w_lo = (w_r - w_hi.astype(F32)).astype(BF16)
    xn, cmb = _router(h, gain, w_hi, w_lo)
    return _moe_dense(xn, cmb, h, w_gate.astype(BF16), w_up.astype(BF16), w_down.astype(BF16))


def kernel(x, attn_norm, w_qkv_a, w_o_a, kv_norm, w_kvf, b_f, w_q_b, w_o_b, moe_norm,
           w_group, w_router, w_gate, w_up, w_down, final_norm):
    batch, seq, d = x.shape
    n = batch * seq
    q_scale = HEAD_DIM ** -0.5 * LOG2_E
    h = x.reshape(n, d)
    n_a = w_qkv_a.shape[0]
    depth = attn_norm.shape[0]
    kv = c_blocks = None
    for layer in range(depth):
        if layer < n_a:
            w = w_qkv_a[layer]
            w = jnp.concatenate([w[:, :d] * q_scale, w[:, d:]], axis=1).astype(BF16)
            qkv = _norm_matmul(h, attn_norm[layer], w, 1024)
            mix = _sb_attention(qkv.reshape(batch, seq, 3 * d), batch, seq)
            h = _matmul_residual(mix.reshape(n, d), w_o_a[layer].astype(BF16), h)
        else:
            j = layer - n_a
            if j == 0:
                kv = _norm_matmul(h, kv_norm, w_kvf[:, :2 * d].astype(BF16), 1024)
                w_f_t = w_kvf[:, 2 * d:].T.astype(BF16)
                cum = _forget_gates(h.reshape(batch, seq, d), kv_norm, w_f_t, b_f)
                t = ATTN_TILE
                c_blocks = cum.reshape(batch, HEAD_STEPS, HEADS_PER_STEP, seq // t, t).transpose(0, 1, 3, 2, 4)
                kv = kv.reshape(batch, seq, 2 * d)
            q = _norm_matmul(h, attn_norm[layer], (w_q_b[j] * q_scale).astype(BF16), 1024)
            mix = _fox_attention(q.reshape(batch, seq, d), kv, c_blocks, batch, seq)
            h = _matmul_residual(mix.reshape(n, d), w_o_b[j].astype(BF16), h)
        h = _moe_layer(h, moe_norm[layer], w_group[layer], w_router[layer],
                       w_gate[layer], w_up[layer], w_down[layer])
    return _final_norm(h, final_norm).reshape(batch, seq, d)
```

```python
import jax
import jax.numpy as jnp
from jax import lax
from jax.experimental import pallas as pl
from jax.experimental.pallas import tpu as pltpu

D_MODEL = 1024
N_HEADS = 16
HEAD_DIM = 64
N_GROUPS = 4
EXPERTS_PER_GROUP = 4
N_EXPERTS = 16
D_EXPERT = 512
RMS_EPS = 1e-6

LANES = 128
HEADS_PER_STEP = 4
HEAD_LANES = HEADS_PER_STEP * HEAD_DIM
HEAD_STEPS = N_HEADS // HEADS_PER_STEP
LOG2_E = 1.4426950408889634
ROW_TILE = 512
ATTN_TILE = 256
MOE_TILE = 256
VMEM_LIMIT = 48 * 1024 * 1024

F32 = jnp.float32
BF16 = jnp.bfloat16
NT_DIMS = (((1,), (1,)), ((), ()))


def _params(*semantics):
    return pltpu.CompilerParams(dimension_semantics=semantics, vmem_limit_bytes=VMEM_LIMIT)


def _rms_normed(x, gain):
    var = jnp.mean(x * x, axis=-1, keepdims=True)
    return x * lax.rsqrt(var + RMS_EPS) * gain


def _norm_matmul_kernel(h_ref, g_ref, w_ref, o_ref):
    xn = _rms_normed(h_ref[...], g_ref[...]).astype(BF16)
    o_ref[...] = jnp.dot(xn, w_ref[...], preferred_element_type=F32).astype(o_ref.dtype)


def _norm_matmul(h, gain, w, col_tile):
    n, d = h.shape
    n_out = w.shape[1]
    return pl.pallas_call(
        _norm_matmul_kernel,
        grid=(n_out // col_tile, n // ROW_TILE),
        in_specs=[
            pl.BlockSpec((ROW_TILE, d), lambda j, i: (i, 0)),
            pl.BlockSpec((1, d), lambda j, i: (0, 0)),
            pl.BlockSpec((d, col_tile), lambda j, i: (0, j)),
        ],
        out_specs=pl.BlockSpec((ROW_TILE, col_tile), lambda j, i: (i, j)),
        out_shape=jax.ShapeDtypeStruct((n, n_out), BF16),
        compiler_params=_params("parallel", "arbitrary"),
        name="norm_matmul",
    )(h, gain.reshape(1, d), w)


def _matmul_residual_kernel(mix_ref, w_ref, h_ref, o_ref):
    o_ref[...] = h_ref[...] + jnp.dot(mix_ref[...], w_ref[...], preferred_element_type=F32)


def _matmul_residual(mix, w, h):
    n, d = h.shape
    return pl.pallas_call(
        _matmul_residual_kernel,
        grid=(n // ROW_TILE,),
        in_specs=[
            pl.BlockSpec((ROW_TILE, d), lambda i: (i, 0)),
            pl.BlockSpec((d, d), lambda i: (0, 0)),
            pl.BlockSpec((ROW_TILE, d), lambda i: (i, 0)),
        ],
        out_specs=pl.BlockSpec((ROW_TILE, d), lambda i: (i, 0)),
        out_shape=jax.ShapeDtypeStruct((n, d), F32),
        compiler_params=_params("parallel"),
        name="matmul_residual",
    )(mix, w, h)


def _head_ids(rows):
    lane = lax.broadcasted_iota(jnp.int32, (rows, HEAD_LANES), 1)
    return lane // HEAD_DIM


def _only_head(x, head_ids, h):
    return jnp.where(head_ids == h, x, jnp.zeros_like(x))


def _per_head_lanes(head_ids, cols):
    out = cols[-1]
    for h in range(HEADS_PER_STEP - 2, -1, -1):
        out = jnp.where(head_ids == h, cols[h], out)
    return out


def _sb_attn_kernel(q_ref, k_ref, v_ref, o_ref):
    t = ATTN_TILE
    i = pl.program_id(2)
    head_ids = _head_ids(t)
    q = q_ref[0]
    q_heads = [_only_head(q, head_ids, h) for h in range(HEADS_PER_STEP)]
    row = lax.broadcasted_iota(jnp.int32, (t, t), 0)
    col = lax.broadcasted_iota(jnp.int32, (t, t), 1)
    strict = col < row
    neg_suffix_ones = jnp.where(row >= col, -1.0, 0.0).astype(BF16)

    def scores(j):
        kb = k_ref[0, pl.ds(pl.multiple_of(j * t, t), t), :]
        return tuple(lax.dot_general(qh, kb, NT_DIMS, preferred_element_type=F32) for qh in q_heads)

    def weights(zs, cs, diagonal):
        ws, new_cs = [], []
        for z, c in zip(zs, cs):
            sp = jnp.maximum(z, 0.0) + jnp.log(1.0 + jnp.exp2(-jnp.abs(z))) * LOG2_E
            if diagonal:
                sp = jnp.where(strict, sp, 0.0)
            suffix = jnp.dot(sp.astype(BF16), neg_suffix_ones, preferred_element_type=F32)
            w = jnp.exp2(z + suffix + c)
            if diagonal:
                w = jnp.where(strict, w, 0.0)
            ws.append(w.astype(BF16))
            new_cs.append(c - jnp.sum(sp, axis=1, keepdims=True))
        return tuple(ws), tuple(new_cs)

    def add_pv(acc, ws, j):
        vb = v_ref[0, pl.ds(pl.multiple_of(j * t, t), t), :]
        for h, w in enumerate(ws):
            acc = acc + jnp.dot(w, _only_head(vb, head_ids, h), preferred_element_type=F32)
        return acc

    def trip(n, carry):
        ws_prev, cs, acc = carry
        j = i - 1 - n
        acc = add_pv(acc, ws_prev, j + 1)
        ws, cs = weights(scores(j), cs, False)
        return ws, cs, acc

    zeros = jnp.zeros((t, 1), F32)
    ws, cs = weights(scores(i), (zeros,) * HEADS_PER_STEP, True)
    ws, _, acc = lax.fori_loop(0, i, trip, (ws, cs, jnp.zeros((t, HEAD_LANES), F32)))
    o_ref[0] = add_pv(acc, ws, 0).astype(o_ref.dtype)


def _sb_attention(qkv, batch, seq):
    t = ATTN_TILE
    k_off = D_MODEL // HEAD_LANES
    return pl.pallas_call(
        _sb_attn_kernel,
        grid=(batch, HEAD_STEPS, seq // t),
        in_specs=[
            pl.BlockSpec((1, t, HEAD_LANES), lambda b, p, i: (b, i, p)),
            pl.BlockSpec((1, seq, HEAD_LANES), lambda b, p, i: (b, 0, k_off + p)),
            pl.BlockSpec((1, seq, HEAD_LANES), lambda b, p, i: (b, 0, 2 * k_off + p)),
        ],
        out_specs=pl.BlockSpec((1, t, HEAD_LANES), lambda b, p, i: (b, i, p)),
        out_shape=jax.ShapeDtypeStruct((batch, seq, D_MODEL), BF16),
        compiler_params=_params("parallel", "parallel", "arbitrary"),
        name="sb_attention",
    )(qkv, qkv, qkv)


def _forget_gate_kernel(h_ref, g_ref, wf_ref, bf_ref, o_ref):
    t = ATTN_TILE
    seq = h_ref.shape[1]
    u = _rms_normed(h_ref[0], g_ref[...]).astype(BF16)
    logits = lax.dot_general(wf_ref[...], u, NT_DIMS, preferred_element_type=F32) + bf_ref[...]
    log_f = jnp.minimum(logits, 0.0) - jnp.log(1.0 + jnp.exp(-jnp.abs(logits)))
    row = lax.broadcasted_iota(jnp.int32, (t, t), 0)
    col = lax.broadcasted_iota(jnp.int32, (t, t), 1)
    prefix_ones = (row <= col).astype(BF16)
    carry = jnp.zeros((N_HEADS, 1), F32)
    for n in range(seq // t):
        blk = log_f[:, n * t:(n + 1) * t]
        hi = blk.astype(BF16)
        lo = (blk - hi.astype(F32)).astype(BF16)
        cs = (jnp.dot(hi, prefix_ones, preferred_element_type=F32)
              + jnp.dot(lo, prefix_ones, preferred_element_type=F32) + carry)
        o_ref[0, :, n * t:(n + 1) * t] = cs * LOG2_E
        carry = cs[:, t - 1:t]


def _forget_gates(h3, gain, w_f_t, b_f):
    batch, seq, d = h3.shape
    return pl.pallas_call(
        _forget_gate_kernel,
        grid=(batch,),
        in_specs=[
            pl.BlockSpec((1, seq, d), lambda b: (b, 0, 0)),
            pl.BlockSpec((1, d), lambda b: (0, 0)),
            pl.BlockSpec((N_HEADS, d), lambda b: (0, 0)),
            pl.BlockSpec((N_HEADS, 1), lambda b: (0, 0)),
        ],
        out_specs=pl.BlockSpec((1, N_HEADS, seq), lambda b: (b, 0, 0)),
        out_shape=jax.ShapeDtypeStruct((batch, N_HEADS, seq), F32),
        compiler_params=_params("parallel"),
        name="forget_gates",
    )(h3, gain.reshape(1, d), w_f_t, b_f.reshape(N_HEADS, 1))


def _fox_attn_kernel(q_ref, k_ref, v_ref, c_ref, o_ref):
    t = ATTN_TILE
    i = pl.program_id(2)
    head_ids = _head_ids(t)
    q = q_ref[0]
    q_heads = [_only_head(q, head_ids, h) for h in range(HEADS_PER_STEP)]
    row = lax.broadcasted_iota(jnp.int32, (t, t), 0)
    col = lax.broadcasted_iota(jnp.int32, (t, t), 1)
    causal = col <= row

    def scores(j):
        kb = k_ref[0, pl.ds(pl.multiple_of(j * t, t), t), :]
        cb = c_ref[0, 0, j]
        return tuple(lax.dot_general(qh, kb, NT_DIMS, preferred_element_type=F32) - cb[h:h + 1, :]
                     for h, qh in enumerate(q_heads))

    def probs(ss, ms, ls, diagonal):
        ps, new_ms, new_ls, alphas = [], [], [], []
        for s, m, l in zip(ss, ms, ls):
            if diagonal:
                s = jnp.where(causal, s, -jnp.inf)
            m_new = jnp.maximum(m, jnp.max(s, axis=1, keepdims=True))
            p = jnp.exp2(s - m_new)
            alpha = jnp.exp2(m - m_new)
            ps.append(p.astype(BF16))
            new_ms.append(m_new)
            new_ls.append(alpha * l + jnp.sum(p, axis=1, keepdims=True))
            alphas.append(alpha)
        return tuple(ps), tuple(new_ms), tuple(new_ls), alphas

    def add_pv(acc, ps, j):
        vb = v_ref[0, pl.ds(pl.multiple_of(j * t, t), t), :]
        for h, p in enumerate(ps):
            acc = acc + jnp.dot(p, _only_head(vb, head_ids, h), preferred_element_type=F32)
        return acc

    def trip(n, carry):
        ps_prev, ms, ls, acc = carry
        j = i - 1 - n
        acc = add_pv(acc, ps_prev, j + 1)
        ps, ms, ls, alphas = probs(scores(j), ms, ls, False)
        return ps, ms, ls, acc * _per_head_lanes(head_ids, alphas)

    neg = jnp.full((t, 1), -jnp.inf, F32)
    zeros = jnp.zeros((t, 1), F32)
    ps, ms, ls, _ = probs(scores(i), (neg,) * HEADS_PER_STEP, (zeros,) * HEADS_PER_STEP, True)
    carry = (ps, ms, ls, jnp.zeros((t, HEAD_LANES), F32))
    ps, _, ls, acc = lax.fori_loop(0, i, trip, carry)
    o_ref[0] = (add_pv(acc, ps, 0) / _per_head_lanes(head_ids, ls)).astype(o_ref.dtype)


def _fox_attention(q, kv, c_blocks, batch, seq):
    t = ATTN_TILE
    v_off = D_MODEL // HEAD_LANES
    return pl.pallas_call(
        _fox_attn_kernel,
        grid=(batch, HEAD_STEPS, seq // t),
        in_specs=[
            pl.BlockSpec((1, t, HEAD_LANES), lambda b, p, i: (b, i, p)),
            pl.BlockSpec((1, seq, HEAD_LANES), lambda b, p, i: (b, 0, p)),
            pl.BlockSpec((1, seq, HEAD_LANES), lambda b, p, i: (b, 0, v_off + p)),
            pl.BlockSpec((1, 1, seq // t, HEADS_PER_STEP, t), lambda b, p, i: (b, p, 0, 0, 0)),
        ],
        out_specs=pl.BlockSpec((1, t, HEAD_LANES), lambda b, p, i: (b, i, p)),
        out_shape=jax.ShapeDtypeStruct((batch, seq, D_MODEL), BF16),
        compiler_params=_params("parallel", "parallel", "arbitrary"),
        name="fox_attention",
    )(q, kv, kv, c_blocks)


def _first_index_of_max(x, lane, valid):
    masked = jnp.where(valid, x, -jnp.inf)
    top = jnp.max(masked, axis=1, keepdims=True)
    idx = jnp.min(jnp.where(masked == top, lane, LANES), axis=1, keepdims=True)
    return top, idx


def _router_kernel(h_ref, g_ref, whi_ref, wlo_ref, sel_ref, wts_ref):
    xn = _rms_normed(h_ref[...], g_ref[...])
    x_hi = xn.astype(BF16)
    x_lo = (xn - x_hi.astype(F32)).astype(BF16)
    logits = (jnp.dot(x_hi, whi_ref[...], preferred_element_type=F32)
              + jnp.dot(x_lo, whi_ref[...], preferred_element_type=F32)
              + jnp.dot(x_hi, wlo_ref[...], preferred_element_type=F32))
    lane = lax.broadcasted_iota(jnp.int32, logits.shape, 1)
    is_group = (lane >= N_EXPERTS) & (lane < N_EXPERTS + N_GROUPS)
    g_top, g_lane = _first_index_of_max(logits, lane, is_group)
    g_sum = jnp.sum(jnp.where(is_group, jnp.exp(logits - g_top), 0.0), axis=1, keepdims=True)
    g_prob = 1.0 / g_sum
    g_idx = g_lane - N_EXPERTS
    in_group = (lane >= g_idx * EXPERTS_PER_GROUP) & (lane < (g_idx + 1) * EXPERTS_PER_GROUP)
    t1, i1 = _first_index_of_max(logits, lane, in_group)
    t2, i2 = _first_index_of_max(logits, lane, in_group & (lane != i1))
    e2 = jnp.exp(t2 - t1)
    w1 = g_prob / (1.0 + e2)
    w2 = g_prob * e2 / (1.0 + e2)
    sel_ref[...] = jnp.where(lane == 0, i1, i2)
    wts_ref[...] = jnp.where(lane == 0, w1, w2)


def _router(h, gain, w_hi, w_lo):
    n, d = h.shape
    return pl.pallas_call(
        _router_kernel,
        grid=(n // ROW_TILE,),
        in_specs=[
            pl.BlockSpec((ROW_TILE, d), lambda i: (i, 0)),
            pl.BlockSpec((1, d), lambda i: (0, 0)),
            pl.BlockSpec((d, LANES), lambda i: (0, 0)),
            pl.BlockSpec((d, LANES), lambda i: (0, 0)),
        ],
        out_specs=[
            pl.BlockSpec((ROW_TILE, LANES), lambda i: (i, 0)),
            pl.BlockSpec((ROW_TILE, LANES), lambda i: (i, 0)),
        ],
        out_shape=[
            jax.ShapeDtypeStruct((n, LANES), jnp.int32),
            jax.ShapeDtypeStruct((n, LANES), F32),
        ],
        compiler_params=_params("parallel"),
        name="moe_router",
    )(h, gain.reshape(1, d), w_hi, w_lo)


def _row_gather(src_hbm, dst, sem, index_of_row, rows):
    def start_row(r, _):
        pltpu.make_async_copy(src_hbm.at[pl.ds(index_of_row(r), 1)], dst.at[pl.ds(r, 1)], sem).start()
        return 0
    lax.fori_loop(0, rows, start_row, 0, unroll=8)


def _wait_rows(src_hbm, dst, sem, rows):
    pltpu.make_async_copy(src_hbm.at[pl.ds(0, rows)], dst, sem).wait()


def _moe_sparse_kernel(tile_expert_ref, row_token_ref, h_hbm, g_ref, scale_ref,
                       wg_ref, wu_ref, wd_ref, o_ref, xbuf, sems):
    del tile_expert_ref
    t = pl.program_id(0)
    slot = t % 2

    def start_tile(tile, into):
        base = tile * MOE_TILE
        _row_gather(h_hbm, xbuf.at[into], sems.at[into], lambda r: row_token_ref[base + r], MOE_TILE)

    @pl.when(t == 0)
    def _():
        start_tile(0, 0)

    @pl.when(t + 1 < pl.num_programs(0))
    def _():
        start_tile(t + 1, 1 - slot)

    _wait_rows(h_hbm, xbuf.at[slot], sems.at[slot], MOE_TILE)
    x = _rms_normed(xbuf[slot], g_ref[...]).astype(BF16)
    gate = jnp.dot(x, wg_ref[0], preferred_element_type=F32)
    up = jnp.dot(x, wu_ref[0], preferred_element_type=F32)
    act = gate / (1.0 + jnp.exp(-gate)) * up * scale_ref[...]
    o_ref[...] = jnp.dot(act.astype(BF16), wd_ref[0], preferred_element_type=F32)


def _moe_sparse(tile_expert, row_token, h, gain, row_scale, w_gate, w_up, w_down):
    n, d = h.shape
    f = w_gate.shape[2]
    rows = row_token.shape[0]
    grid_spec = pltpu.PrefetchScalarGridSpec(
        num_scalar_prefetch=2,
        grid=(rows // MOE_TILE,),
        in_specs=[
            pl.BlockSpec(memory_space=pl.ANY),
            pl.BlockSpec((1, d), lambda t, te, rt: (0, 0)),
            pl.BlockSpec((MOE_TILE, 1), lambda t, te, rt: (t, 0)),
            pl.BlockSpec((1, d, f), lambda t, te, rt: (te[t], 0, 0)),
            pl.BlockSpec((1, d, f), lambda t, te, rt: (te[t], 0, 0)),
            pl.BlockSpec((1, f, d), lambda t, te, rt: (te[t], 0, 0)),
        ],
        out_specs=pl.BlockSpec((MOE_TILE, d), lambda t, te, rt: (t, 0)),
        scratch_shapes=[pltpu.VMEM((2, MOE_TILE, d), F32), pltpu.SemaphoreType.DMA((2,))],
    )
    return pl.pallas_call(
        _moe_sparse_kernel,
        grid_spec=grid_spec,
        out_shape=jax.ShapeDtypeStruct((rows, d), F32),
        compiler_params=_params("arbitrary"),
        name="moe_experts",
    )(tile_expert, row_token, h, gain.reshape(1, d), row_scale, w_gate, w_up, w_down)


def _moe_combine_kernel(pos_ref, h_ref, y_hbm, o_ref, ybuf, sems):
    t = pl.program_id(0)
    slot = t % 2

    def start_tile(tile, into):
        base = tile * MOE_TILE
        for k in range(2):
            _row_gather(y_hbm, ybuf.at[into, k], sems.at[into, k],
                        lambda r: pos_ref[2 * (base + r) + k], MOE_TILE)

    @pl.when(t == 0)
    def _():
        start_tile(0, 0)

    @pl.when(t + 1 < pl.num_programs(0))
    def _():
        start_tile(t + 1, 1 - slot)

    for k in range(2):
        _wait_rows(y_hbm, ybuf.at[slot, k], sems.at[slot, k], MOE_TILE)
    o_ref[...] = h_ref[...] + ybuf[slot, 0] + ybuf[slot, 1]


def _moe_combine(pos, h, y):
    n, d = h.shape
    grid_spec = pltpu.PrefetchScalarGridSpec(
        num_scalar_prefetch=1,
        grid=(n // MOE_TILE,),
        in_specs=[
            pl.BlockSpec((MOE_TILE, d), lambda t, p: (t, 0)),
            pl.BlockSpec(memory_space=pl.ANY),
        ],
        out_specs=pl.BlockSpec((MOE_TILE, d), lambda t, p: (t, 0)),
        scratch_shapes=[pltpu.VMEM((2, 2, MOE_TILE, d), F32), pltpu.SemaphoreType.DMA((2, 2))],
    )
    return pl.pallas_call(
        _moe_combine_kernel,
        grid_spec=grid_spec,
        out_shape=jax.ShapeDtypeStruct((n, d), F32),
        compiler_params=_params("arbitrary"),
        name="moe_combine",
    )(pos, h, y)


def _dispatch_plan(sel, wts, n):
    experts = sel[:, :2].reshape(2 * n)
    weights = wts[:, :2].reshape(2 * n)
    onehot = (experts[:, None] == jnp.arange(N_EXPERTS, dtype=jnp.int32)[None, :]).astype(jnp.int32)
    counts = jnp.sum(onehot, axis=0)
    rank = jnp.sum((jnp.cumsum(onehot, axis=0) - onehot) * onehot, axis=1)
    padded = (counts + MOE_TILE - 1) // MOE_TILE * MOE_TILE
    seg_end = jnp.cumsum(padded)
    seg_start = seg_end - padded
    pos = seg_start[experts] + rank
    rows = 2 * n + N_EXPERTS * MOE_TILE
    tile_start = jnp.arange(rows // MOE_TILE, dtype=jnp.int32) * MOE_TILE
    tile_expert = jnp.minimum(jnp.searchsorted(seg_end, tile_start, side="right"), N_EXPERTS - 1)
    order = jnp.argsort(experts * (2 * n) + jnp.arange(2 * n, dtype=jnp.int32))
    dense_start = jnp.cumsum(counts) - counts
    row = jnp.arange(rows, dtype=jnp.int32)
    row_expert = jnp.repeat(tile_expert, MOE_TILE)
    within = row - seg_start[row_expert]
    valid = within < counts[row_expert]
    source = order[jnp.clip(dense_start[row_expert] + within, 0, 2 * n - 1)]
    row_token = jnp.where(valid, source // 2, 0).astype(jnp.int32)
    row_scale = jnp.where(valid, weights[source], 0.0).reshape(rows, 1)
    return tile_expert.astype(jnp.int32), row_token, row_scale, pos.astype(jnp.int32)


def _rms_norm_kernel(h_ref, g_ref, o_ref):
    o_ref[...] = _rms_normed(h_ref[...], g_ref[...])


def _final_norm(h, gain):
    n, d = h.shape
    return pl.pallas_call(
        _rms_norm_kernel,
        grid=(n // ROW_TILE,),
        in_specs=[
            pl.BlockSpec((ROW_TILE, d), lambda i: (i, 0)),
            pl.BlockSpec((1, d), lambda i: (0, 0)),
        ],
        out_specs=pl.BlockSpec((ROW_TILE, d), lambda i: (i, 0)),
        out_shape=jax.ShapeDtypeStruct((n, d), F32),
        compiler_params=_params("parallel"),
        name="final_norm",
    )(h, gain.reshape(1, d))


def _moe_layer(h, gain, w_group, w_router, w_gate, w_up, w_down):
    d = h.shape[1]
    w_r = jnp.zeros((d, LANES), F32)
    w_r = w_r.at[:, :N_EXPERTS].set(w_router).at[:, N_EXPERTS:N_EXPERTS + N_GROUPS].set(w_group)
    w_hi = w_r.astype(BF16)
    w_lo = (w_r - w_hi.astype(F32)).astype(BF16)
    sel, wts = _router(h, gain, w_hi, w_lo)
    tile_expert, row_token, row_scale, pos = _dispatch_plan(sel, wts, h.shape[0])
    y = _moe_sparse(tile_expert, row_token, h, gain, row_scale,
                    w_gate.astype(BF16), w_up.astype(BF16), w_down.astype(BF16))
    return _moe_combine(pos, h, y)


def kernel(x, attn_norm, w_qkv_a, w_o_a, kv_norm, w_kvf, b_f, w_q_b, w_o_b, moe_norm,
           w_group, w_router, w_gate, w_up, w_down, final_norm):
    batch, seq, d = x.shape
    n = batch * seq
    q_scale = HEAD_DIM ** -0.5 * LOG2_E
    h = x.reshape(n, d)
    n_a = w_qkv_a.shape[0]
    depth = attn_norm.shape[0]
    kv = c_blocks = None
    for layer in range(depth):
        if layer < n_a:
            w = w_qkv_a[layer]
            w = jnp.concatenate([w[:, :d] * q_scale, w[:, d:]], axis=1).astype(BF16)
            qkv = _norm_matmul(h, attn_norm[layer], w, 1024)
            mix = _sb_attention(qkv.reshape(batch, seq, 3 * d), batch, seq)
            h = _matmul_residual(mix.reshape(n, d), w_o_a[layer].astype(BF16), h)
        else:
            j = layer - n_a
            if j == 0:
                kv = _norm_matmul(h, kv_norm, w_kvf[:, :2 * d].astype(BF16), 1024)
                w_f_t = w_kvf[:, 2 * d:].T.astype(BF16)
                cum = _forget_gates(h.reshape(batch, seq, d), kv_norm, w_f_t, b_f)
                t = ATTN_TILE
                c_blocks = cum.reshape(batch, HEAD_STEPS, HEADS_PER_STEP, seq // t, t).transpose(0, 1, 3, 2, 4)
                kv = kv.reshape(batch, seq, 2 * d)
            q = _norm_matmul(h, attn_norm[layer], (w_q_b[j] * q_scale).astype(BF16), 1024)
            mix = _fox_attention(q.reshape(batch, seq, d), kv, c_blocks, batch, seq)
            h = _matmul_residual(mix.reshape(n, d), w_o_b[j].astype(BF16), h)
        h = _moe_layer(h, moe_norm[layer], w_group[layer], w_router[layer],
                       w_gate[layer], w_up[layer], w_down[layer])
    return _final_norm(h, final_norm).reshape(batch, seq, d)
```

```python
import jax
import jax.numpy as jnp
from jax import lax
from jax.experimental import pallas as pl
from jax.experimental.pallas import tpu as pltpu

D_MODEL = 1024
N_HEADS = 16
HEAD_DIM = 64
N_GROUPS = 4
EXPERTS_PER_GROUP = 4
N_EXPERTS = 16
D_EXPERT = 512
RMS_EPS = 1e-6

LANES = 128
MXU_LANES = 256
HEADS_PER_GROUP = MXU_LANES // HEAD_DIM
GROUPS_PER_STEP = 1
HEADS_PER_STEP = HEADS_PER_GROUP * GROUPS_PER_STEP
STEP_LANES = MXU_LANES * GROUPS_PER_STEP
HEAD_STEPS = N_HEADS // HEADS_PER_STEP
LOG2_E = 1.4426950408889634
SOFTPLUS_CLAMP = 64.0
ROW_TILE = 512
MOE_ROW_TILE = 1024
ATTN_TILE = 256
VMEM_LIMIT = 48 * 1024 * 1024

F32 = jnp.float32
BF16 = jnp.bfloat16
NT_DIMS = (((1,), (1,)), ((), ()))


def _params(*semantics):
    return pltpu.CompilerParams(dimension_semantics=semantics, vmem_limit_bytes=VMEM_LIMIT)


def _rms_normed(x, gain):
    var = jnp.mean(x * x, axis=-1, keepdims=True)
    return x * lax.rsqrt(var + RMS_EPS) * gain


def _norm_matmul_kernel(h_ref, g_ref, w_ref, o_ref):
    xn = _rms_normed(h_ref[...], g_ref[...]).astype(BF16)
    o_ref[...] = jnp.dot(xn, w_ref[...], preferred_element_type=F32).astype(o_ref.dtype)


def _norm_matmul(h, gain, w, col_tile):
    n, d = h.shape
    n_out = w.shape[1]
    return pl.pallas_call(
        _norm_matmul_kernel,
        grid=(n_out // col_tile, n // ROW_TILE),
        in_specs=[
            pl.BlockSpec((ROW_TILE, d), lambda j, i: (i, 0)),
            pl.BlockSpec((1, d), lambda j, i: (0, 0)),
            pl.BlockSpec((d, col_tile), lambda j, i: (0, j)),
        ],
        out_specs=pl.BlockSpec((ROW_TILE, col_tile), lambda j, i: (i, j)),
        out_shape=jax.ShapeDtypeStruct((n, n_out), BF16),
        compiler_params=_params("parallel", "arbitrary"),
        name="norm_matmul",
    )(h, gain.reshape(1, d), w)


def _matmul_residual_kernel(mix_ref, w_ref, h_ref, o_ref):
    o_ref[...] = h_ref[...] + jnp.dot(mix_ref[...], w_ref[...], preferred_element_type=F32)


def _matmul_residual(mix, w, h):
    n, d = h.shape
    return pl.pallas_call(
        _matmul_residual_kernel,
        grid=(n // ROW_TILE,),
        in_specs=[
            pl.BlockSpec((ROW_TILE, d), lambda i: (i, 0)),
            pl.BlockSpec((d, d), lambda i: (0, 0)),
            pl.BlockSpec((ROW_TILE, d), lambda i: (i, 0)),
        ],
        out_specs=pl.BlockSpec((ROW_TILE, d), lambda i: (i, 0)),
        out_shape=jax.ShapeDtypeStruct((n, d), F32),
        compiler_params=_params("parallel"),
        name="matmul_residual",
    )(mix, w, h)


def _head_ids(rows):
    lane = lax.broadcasted_iota(jnp.int32, (rows, MXU_LANES), 1)
    return lane // HEAD_DIM


def _only_head(x, head_ids, h):
    return jnp.where(head_ids == h, x, jnp.zeros_like(x))


def _group_lanes(g):
    return slice(g * MXU_LANES, (g + 1) * MXU_LANES)


def _masked_queries(q_ref, head_ids):
    return [_only_head(q_ref[0, :, _group_lanes(g)], head_ids, h)
            for g in range(GROUPS_PER_STEP) for h in range(HEADS_PER_GROUP)]


def _key_scores(q_heads, k_ref, rows):
    out = []
    for g in range(GROUPS_PER_STEP):
        kb = k_ref[0, rows, _group_lanes(g)]
        for h in range(HEADS_PER_GROUP):
            out.append(lax.dot_general(q_heads[g * HEADS_PER_GROUP + h], kb, NT_DIMS,
                                       preferred_element_type=F32))
    return out


def _add_weighted_values(accs, ws, v_ref, rows, head_ids):
    out = []
    for g in range(GROUPS_PER_STEP):
        vb = v_ref[0, rows, _group_lanes(g)]
        acc = accs[g]
        for h in range(HEADS_PER_GROUP):
            acc = acc + jnp.dot(ws[g * HEADS_PER_GROUP + h], _only_head(vb, head_ids, h),
                                preferred_element_type=F32)
        out.append(acc)
    return tuple(out)


def _per_head_lanes(head_ids, cols):
    out = cols[-1]
    for h in range(HEADS_PER_GROUP - 2, -1, -1):
        out = jnp.where(head_ids == h, cols[h], out)
    return out


def _block_rows(j):
    return pl.ds(pl.multiple_of(j * ATTN_TILE, ATTN_TILE), ATTN_TILE)


def _sb_attn_kernel(q_ref, k_ref, v_ref, o_ref):
    t = ATTN_TILE
    i = pl.program_id(2)
    head_ids = _head_ids(t)
    q_heads = _masked_queries(q_ref, head_ids)
    row = lax.broadcasted_iota(jnp.int32, (t, t), 0)
    col = lax.broadcasted_iota(jnp.int32, (t, t), 1)
    strict = col < row
    neg_suffix_ones = jnp.where(row >= col, -1.0, 0.0).astype(BF16)

    def weights(zs, cs, diagonal):
        ws, new_cs = [], []
        for z, c in zip(zs, cs):
            sp = jnp.maximum(z, jnp.log(1.0 + jnp.exp2(jnp.minimum(z, SOFTPLUS_CLAMP))) * LOG2_E)
            if diagonal:
                sp = jnp.where(strict, sp, 0.0)
            suffix = jnp.dot(sp.astype(BF16), neg_suffix_ones, preferred_element_type=F32)
            w = jnp.exp2(z + suffix + c)
            if diagonal:
                w = jnp.where(strict, w, 0.0)
            ws.append(w.astype(BF16))
            new_cs.append(c - jnp.sum(sp, axis=1, keepdims=True))
        return tuple(ws), tuple(new_cs)

    def trip(n, carry):
        ws_prev, cs, accs = carry
        j = i - 1 - n
        accs = _add_weighted_values(accs, ws_prev, v_ref, _block_rows(j + 1), head_ids)
        ws, cs = weights(_key_scores(q_heads, k_ref, _block_rows(j)), cs, False)
        return ws, cs, accs

    zeros = jnp.zeros((t, 1), F32)
    ws, cs = weights(_key_scores(q_heads, k_ref, _block_rows(i)), (zeros,) * HEADS_PER_STEP, True)
    accs = (jnp.zeros((t, MXU_LANES), F32),) * GROUPS_PER_STEP
    ws, _, accs = lax.fori_loop(0, i, trip, (ws, cs, accs))
    accs = _add_weighted_values(accs, ws, v_ref, _block_rows(0), head_ids)
    for g in range(GROUPS_PER_STEP):
        o_ref[0, :, _group_lanes(g)] = accs[g].astype(o_ref.dtype)


def _sb_attention(qkv, batch, seq):
    t = ATTN_TILE
    k_off = D_MODEL // STEP_LANES
    return pl.pallas_call(
        _sb_attn_kernel,
        grid=(batch, HEAD_STEPS, seq // t),
        in_specs=[
            pl.BlockSpec((1, t, STEP_LANES), lambda b, p, i: (b, i, p)),
            pl.BlockSpec((1, seq, STEP_LANES), lambda b, p, i: (b, 0, k_off + p)),
            pl.BlockSpec((1, seq, STEP_LANES), lambda b, p, i: (b, 0, 2 * k_off + p)),
        ],
        out_specs=pl.BlockSpec((1, t, STEP_LANES), lambda b, p, i: (b, i, p)),
        out_shape=jax.ShapeDtypeStruct((batch, seq, D_MODEL), BF16),
        compiler_params=_params("parallel", "parallel", "arbitrary"),
        name="sb_attention",
    )(qkv, qkv, qkv)


def _forget_gate_kernel(h_ref, g_ref, wf_ref, bf_ref, o_ref):
    t = ATTN_TILE
    seq = h_ref.shape[1]
    u = _rms_normed(h_ref[0], g_ref[...]).astype(BF16)
    logits = lax.dot_general(wf_ref[...], u, NT_DIMS, preferred_element_type=F32) + bf_ref[...]
    log_f = jnp.minimum(logits, 0.0) - jnp.log(1.0 + jnp.exp(-jnp.abs(logits)))
    row = lax.broadcasted_iota(jnp.int32, (t, t), 0)
    col = lax.broadcasted_iota(jnp.int32, (t, t), 1)
    prefix_ones = (row <= col).astype(BF16)
    carry = jnp.zeros((N_HEADS, 1), F32)
    for n in range(seq // t):
        blk = log_f[:, n * t:(n + 1) * t]
        hi = blk.astype(BF16)
        lo = (blk - hi.astype(F32)).astype(BF16)
        cs = (jnp.dot(hi, prefix_ones, preferred_element_type=F32)
              + jnp.dot(lo, prefix_ones, preferred_element_type=F32) + carry)
        o_ref[0, :, n * t:(n + 1) * t] = cs * LOG2_E
        carry = cs[:, t - 1:t]


def _forget_gates(h3, gain, w_f_t, b_f):
    batch, seq, d = h3.shape
    return pl.pallas_call(
        _forget_gate_kernel,
        grid=(batch,),
        in_specs=[
            pl.BlockSpec((1, seq, d), lambda b: (b, 0, 0)),
            pl.BlockSpec((1, d), lambda b: (0, 0)),
            pl.BlockSpec((N_HEADS, d), lambda b: (0, 0)),
            pl.BlockSpec((N_HEADS, 1), lambda b: (0, 0)),
        ],
        out_specs=pl.BlockSpec((1, N_HEADS, seq), lambda b: (b, 0, 0)),
        out_shape=jax.ShapeDtypeStruct((batch, N_HEADS, seq), F32),
        compiler_params=_params("parallel"),
        name="forget_gates",
    )(h3, gain.reshape(1, d), w_f_t, b_f.reshape(N_HEADS, 1))


def _fox_attn_kernel(q_ref, k_ref, v_ref, c_ref, o_ref):
    t = ATTN_TILE
    i = pl.program_id(2)
    head_ids = _head_ids(t)
    q_heads = _masked_queries(q_ref, head_ids)
    row = lax.broadcasted_iota(jnp.int32, (t, t), 0)
    col = lax.broadcasted_iota(jnp.int32, (t, t), 1)
    causal = col <= row

    def scores(j):
        cb = c_ref[0, 0, j]
        qk = _key_scores(q_heads, k_ref, _block_rows(j))
        return [s - cb[h:h + 1, :] for h, s in enumerate(qk)]

    def probs(ss, ms, ls, diagonal):
        ps, new_ms, new_ls, alphas = [], [], [], []
        for s, m, l in zip(ss, ms, ls):
            if diagonal:
                s = jnp.where(causal, s, -jnp.inf)
            m_new = jnp.maximum(m, jnp.max(s, axis=1, keepdims=True))
            p = jnp.exp2(s - m_new)
            alpha = jnp.exp2(m - m_new)
            ps.append(p.astype(BF16))
            new_ms.append(m_new)
            new_ls.append(alpha * l + jnp.sum(p, axis=1, keepdims=True))
            alphas.append(alpha)
        return tuple(ps), tuple(new_ms), tuple(new_ls), alphas

    def group_lanes_of(cols):
        return [_per_head_lanes(head_ids, cols[g * HEADS_PER_GROUP:(g + 1) * HEADS_PER_GROUP])
                for g in range(GROUPS_PER_STEP)]

    def trip(n, carry):
        ps_prev, ms, ls, accs = carry
        j = i - 1 - n
        accs = _add_weighted_values(accs, ps_prev, v_ref, _block_rows(j + 1), head_ids)
        ps, ms, ls, alphas = probs(scores(j), ms, ls, False)
        accs = tuple(acc * a for acc, a in zip(accs, group_lanes_of(alphas)))
        return ps, ms, ls, accs

    neg = jnp.full((t, 1), -jnp.inf, F32)
    zeros = jnp.zeros((t, 1), F32)
    ps, ms, ls, _ = probs(scores(i), (neg,) * HEADS_PER_STEP, (zeros,) * HEADS_PER_STEP, True)
    accs = (jnp.zeros((t, MXU_LANES), F32),) * GROUPS_PER_STEP
    ps, _, ls, accs = lax.fori_loop(0, i, trip, (ps, ms, ls, accs))
    accs = _add_weighted_values(accs, ps, v_ref, _block_rows(0), head_ids)
    for g, denom in enumerate(group_lanes_of(ls)):
        o_ref[0, :, _group_lanes(g)] = (accs[g] / denom).astype(o_ref.dtype)


def _fox_attention(q, kv, c_blocks, batch, seq):
    t = ATTN_TILE
    v_off = D_MODEL // STEP_LANES
    return pl.pallas_call(
        _fox_attn_kernel,
        grid=(batch, HEAD_STEPS, seq // t),
        in_specs=[
            pl.BlockSpec((1, t, STEP_LANES), lambda b, p, i: (b, i, p)),
            pl.BlockSpec((1, seq, STEP_LANES), lambda b, p, i: (b, 0, p)),
            pl.BlockSpec((1, seq, STEP_LANES), lambda b, p, i: (b, 0, v_off + p)),
            pl.BlockSpec((1, 1, seq // t, HEADS_PER_STEP, t), lambda b, p, i: (b, p, 0, 0, 0)),
        ],
        out_specs=pl.BlockSpec((1, t, STEP_LANES), lambda b, p, i: (b, i, p)),
        out_shape=jax.ShapeDtypeStruct((batch, seq, D_MODEL), BF16),
        compiler_params=_params("parallel", "parallel", "arbitrary"),
        name="fox_attention",
    )(q, kv, kv, c_blocks)


def _first_index_of_max(x, lane, valid):
    masked = jnp.where(valid, x, -jnp.inf)
    top = jnp.max(masked, axis=1, keepdims=True)
    idx = jnp.min(jnp.where(masked == top, lane, LANES), axis=1, keepdims=True)
    return top, idx


def _router_kernel(h_ref, g_ref, whi_ref, wlo_ref, xn_ref, cmb_ref):
    xn = _rms_normed(h_ref[...], g_ref[...])
    x_hi = xn.astype(BF16)
    xn_ref[...] = x_hi
    x_lo = (xn - x_hi.astype(F32)).astype(BF16)
    logits = (jnp.dot(x_hi, whi_ref[...], preferred_element_type=F32)
              + jnp.dot(x_lo, whi_ref[...], preferred_element_type=F32)
              + jnp.dot(x_hi, wlo_ref[...], preferred_element_type=F32))
    lane = lax.broadcasted_iota(jnp.int32, logits.shape, 1)
    is_group = (lane >= N_EXPERTS) & (lane < N_EXPERTS + N_GROUPS)
    g_top, g_lane = _first_index_of_max(logits, lane, is_group)
    g_sum = jnp.sum(jnp.where(is_group, jnp.exp(logits - g_top), 0.0), axis=1, keepdims=True)
    g_prob = 1.0 / g_sum
    g_idx = g_lane - N_EXPERTS
    in_group = (lane >= g_idx * EXPERTS_PER_GROUP) & (lane < (g_idx + 1) * EXPERTS_PER_GROUP)
    t1, i1 = _first_index_of_max(logits, lane, in_group)
    t2, i2 = _first_index_of_max(logits, lane, in_group & (lane != i1))
    e2 = jnp.exp(t2 - t1)
    w1 = g_prob / (1.0 + e2)
    w2 = g_prob * e2 / (1.0 + e2)
    cmb_ref[...] = jnp.where(lane == i1, w1, 0.0) + jnp.where(lane == i2, w2, 0.0)


def _router(h, gain, w_hi, w_lo):
    n, d = h.shape
    return pl.pallas_call(
        _router_kernel,
        grid=(n // ROW_TILE,),
        in_specs=[
            pl.BlockSpec((ROW_TILE, d), lambda i: (i, 0)),
            pl.BlockSpec((1, d), lambda i: (0, 0)),
            pl.BlockSpec((d, LANES), lambda i: (0, 0)),
            pl.BlockSpec((d, LANES), lambda i: (0, 0)),
        ],
        out_specs=[
            pl.BlockSpec((ROW_TILE, d), lambda i: (i, 0)),
            pl.BlockSpec((ROW_TILE, LANES), lambda i: (i, 0)),
        ],
        out_shape=[
            jax.ShapeDtypeStruct((n, d), BF16),
            jax.ShapeDtypeStruct((n, LANES), F32),
        ],
        compiler_params=_params("parallel"),
        name="moe_router",
    )(h, gain.reshape(1, d), w_hi, w_lo)


def _moe_dense_kernel(xn_ref, cmb_ref, h_ref, wg_ref, wu_ref, wd_ref, o_ref):
    e = pl.program_id(1)

    @pl.when(e == 0)
    def _():
        o_ref[...] = h_ref[...]

    x = xn_ref[...]
    gate = jnp.dot(x, wg_ref[0], preferred_element_type=F32)
    up = jnp.dot(x, wu_ref[0], preferred_element_type=F32)
    cmb = cmb_ref[...]
    lane = lax.broadcasted_iota(jnp.int32, cmb.shape, 1)
    scale = jnp.sum(jnp.where(lane == e, cmb, 0.0), axis=1, keepdims=True)
    act = gate / (1.0 + jnp.exp(-gate)) * up * scale
    o_ref[...] += jnp.dot(act.astype(BF16), wd_ref[0], preferred_element_type=F32)


def _moe_dense(xn, cmb, h, w_gate, w_up, w_down):
    n, d = h.shape
    f = w_gate.shape[2]
    rows = MOE_ROW_TILE
    return pl.pallas_call(
        _moe_dense_kernel,
        grid=(n // rows, N_EXPERTS),
        in_specs=[
            pl.BlockSpec((rows, d), lambda i, e: (i, 0)),
            pl.BlockSpec((rows, LANES), lambda i, e: (i, 0)),
            pl.BlockSpec((rows, d), lambda i, e: (i, 0)),
            pl.BlockSpec((1, d, f), lambda i, e: (e, 0, 0)),
            pl.BlockSpec((1, d, f), lambda i, e: (e, 0, 0)),
            pl.BlockSpec((1, f, d), lambda i, e: (e, 0, 0)),
        ],
        out_specs=pl.BlockSpec((rows, d), lambda i, e: (i, 0)),
        out_shape=jax.ShapeDtypeStruct((n, d), F32),
        compiler_params=_params("parallel", "arbitrary"),
        name="moe_experts",
    )(xn, cmb, h, w_gate, w_up, w_down)


def _rms_norm_kernel(h_ref, g_ref, o_ref):
    o_ref[...] = _rms_normed(h_ref[...], g_ref[...])


def _final_norm(h, gain):
    n, d = h.shape
    return pl.pallas_call(
        _rms_norm_kernel,
        grid=(n // ROW_TILE,),
        in_specs=[
            pl.BlockSpec((ROW_TILE, d), lambda i: (i, 0)),
            pl.BlockSpec((1, d), lambda i: (0, 0)),
        ],
        out_specs=pl.BlockSpec((ROW_TILE, d), lambda i: (i, 0)),
        out_shape=jax.ShapeDtypeStruct((n, d), F32),
        compiler_params=_params("parallel"),
        name="final_norm",
    )(h, gain.reshape(1, d))


def _moe_layer(h, gain, w_group, w_router, w_gate, w_up, w_down):
    d = h.shape[1]
    w_r = jnp.zeros((d, LANES), F32)
    w_r = w_r.at[:, :N_EXPERTS].set(w_router).at[:, N_EXPERTS:N_EXPERTS + N_GROUPS].set(w_group)
    w_hi = w_r.astype(BF16)
    w_lo = (w_r - w_hi.astype(F32)).astype(BF16)
    xn, cmb = _router(h, gain, w_hi, w_lo)
    return _moe_dense(xn, cmb, h, w_gate.astype(BF16), w_up.astype(BF16), w_down.astype(BF16))


def kernel(x, attn_norm, w_qkv_a, w_o_a, kv_norm, w_kvf, b_f, w_q_b, w_o_b, moe_norm,
           w_group, w_router, w_gate, w_up, w_down, final_norm):
    batch, seq, d = x.shape
    n = batch * seq
    q_scale = HEAD_DIM ** -0.5 * LOG2_E
    h = x.reshape(n, d)
    n_a = w_qkv_a.shape[0]
    depth = attn_norm.shape[0]
    kv = c_blocks = None
    for layer in range(depth):
        if layer < n_a:
            w = w_qkv_a[layer]
            w = jnp.concatenate([w[:, :d] * q_scale, w[:, d:]], axis=1).astype(BF16)
            qkv = _norm_matmul(h, attn_norm[layer], w, 1024)
            mix = _sb_attention(qkv.reshape(batch, seq, 3 * d), batch, seq)
            h = _matmul_residual(mix.reshape(n, d), w_o_a[layer].astype(BF16), h)
        else:
            j = layer - n_a
            if j == 0:
                kv = _norm_matmul(h, kv_norm, w_kvf[:, :2 * d].astype(BF16), 1024)
                w_f_t = w_kvf[:, 2 * d:].T.astype(BF16)
                cum = _forget_gates(h.reshape(batch, seq, d), kv_norm, w_f_t, b_f)
                t = ATTN_TILE
                c_blocks = cum.reshape(batch, HEAD_STEPS, HEADS_PER_STEP, seq // t, t).transpose(0, 1, 3, 2, 4)
                kv = kv.reshape(batch, seq, 2 * d)
            q = _norm_matmul(h, attn_norm[layer], (w_q_b[j] * q_scale).astype(BF16), 1024)
            mix = _fox_attention(q.reshape(batch, seq, d), kv, c_blocks, batch, seq)
            h = _matmul_residual(mix.reshape(n, d), w_o_b[j].astype(BF16), h)
        h = _moe_layer(h, moe_norm[layer], w_group[layer], w_router[layer],
                       w_gate[layer], w_up[layer], w_down[layer])
    return _final_norm(h, final_norm).reshape(batch, seq, d)
```

```python
import jax
import jax.numpy as jnp
from jax import lax
from jax.experimental import pallas as pl
from jax.experimental.pallas import tpu as pltpu

D_MODEL = 1024
N_HEADS = 16
HEAD_DIM = 64
N_GROUPS = 4
EXPERTS_PER_GROUP = 4
N_EXPERTS = 16
D_EXPERT = 512
RMS_EPS = 1e-6

LANES = 128
MXU_LANES = 256
HEADS_PER_GROUP = MXU_LANES // HEAD_DIM
GROUPS_PER_STEP = 1
HEADS_PER_STEP = HEADS_PER_GROUP * GROUPS_PER_STEP
STEP_LANES = MXU_LANES * GROUPS_PER_STEP
HEAD_STEPS = N_HEADS // HEADS_PER_STEP
LOG2_E = 1.4426950408889634
SOFTPLUS_CLAMP = 64.0
ROW_TILE = 512
MOE_ROW_TILE = 1024
ATTN_TILE = 256
VMEM_LIMIT = 48 * 1024 * 1024

F32 = jnp.float32
BF16 = jnp.bfloat16
NT_DIMS = (((1,), (1,)), ((), ()))


def _params(*semantics):
    return pltpu.CompilerParams(dimension_semantics=semantics, vmem_limit_bytes=VMEM_LIMIT)


def _rms_normed(x, gain):
    var = jnp.mean(x * x, axis=-1, keepdims=True)
    return x * lax.rsqrt(var + RMS_EPS) * gain


def _norm_matmul_kernel(h_ref, g_ref, w_ref, o_ref):
    xn = _rms_normed(h_ref[...], g_ref[...]).astype(BF16)
    o_ref[...] = jnp.dot(xn, w_ref[...], preferred_element_type=F32).astype(o_ref.dtype)


def _norm_matmul(h, gain, w, col_tile):
    n, d = h.shape
    n_out = w.shape[1]
    return pl.pallas_call(
        _norm_matmul_kernel,
        grid=(n_out // col_tile, n // ROW_TILE),
        in_specs=[
            pl.BlockSpec((ROW_TILE, d), lambda j, i: (i, 0)),
            pl.BlockSpec((1, d), lambda j, i: (0, 0)),
            pl.BlockSpec((d, col_tile), lambda j, i: (0, j)),
        ],
        out_specs=pl.BlockSpec((ROW_TILE, col_tile), lambda j, i: (i, j)),
        out_shape=jax.ShapeDtypeStruct((n, n_out), BF16),
        compiler_params=_params("parallel", "arbitrary"),
        name="norm_matmul",
    )(h, gain.reshape(1, d), w)


def _matmul_residual_kernel(mix_ref, w_ref, h_ref, o_ref):
    o_ref[...] = h_ref[...] + jnp.dot(mix_ref[...], w_ref[...], preferred_element_type=F32)


def _matmul_residual(mix, w, h):
    n, d = h.shape
    return pl.pallas_call(
        _matmul_residual_kernel,
        grid=(n // ROW_TILE,),
        in_specs=[
            pl.BlockSpec((ROW_TILE, d), lambda i: (i, 0)),
            pl.BlockSpec((d, d), lambda i: (0, 0)),
            pl.BlockSpec((ROW_TILE, d), lambda i: (i, 0)),
        ],
        out_specs=pl.BlockSpec((ROW_TILE, d), lambda i: (i, 0)),
        out_shape=jax.ShapeDtypeStruct((n, d), F32),
        compiler_params=_params("parallel"),
        name="matmul_residual",
    )(mix, w, h)


def _head_ids(rows):
    lane = lax.broadcasted_iota(jnp.int32, (rows, MXU_LANES), 1)
    return lane // HEAD_DIM


def _only_head(x, head_ids, h):
    return jnp.where(head_ids == h, x, jnp.zeros_like(x))


def _group_lanes(g):
    return slice(g * MXU_LANES, (g + 1) * MXU_LANES)


def _masked_queries(q_ref, head_ids):
    return [_only_head(q_ref[0, :, _group_lanes(g)], head_ids, h)
            for g in range(GROUPS_PER_STEP) for h in range(HEADS_PER_GROUP)]


def _key_scores(q_heads, k_ref, rows):
    out = []
    for g in range(GROUPS_PER_STEP):
        kb = k_ref[0, rows, _group_lanes(g)]
        for h in range(HEADS_PER_GROUP):
            out.append(lax.dot_general(q_heads[g * HEADS_PER_GROUP + h], kb, NT_DIMS,
                                       preferred_element_type=F32))
    return out


def _weighted_values(ws_ref, v_ref, rows, head_ids):
    out = []
    for g in range(GROUPS_PER_STEP):
        vb = v_ref[0, rows, _group_lanes(g)]
        acc = None
        for h in range(HEADS_PER_GROUP):
            pv = jnp.dot(ws_ref[g * HEADS_PER_GROUP + h], _only_head(vb, head_ids, h),
                         preferred_element_type=F32)
            acc = pv if acc is None else acc + pv
        out.append(acc)
    return out


def _attn_scratch(n_stats):
    t = ATTN_TILE
    return ([pltpu.VMEM((HEADS_PER_STEP, t, t), BF16)]
            + [pltpu.VMEM((HEADS_PER_STEP, t, LANES), F32)] * n_stats
            + [pltpu.VMEM((GROUPS_PER_STEP, t, MXU_LANES), F32)])


def _lane_tiled(x, width):
    return jnp.tile(x, (1, width // LANES))


def _per_head_lanes(head_ids, cols):
    cols = [_lane_tiled(c, MXU_LANES) for c in cols]
    out = cols[-1]
    for h in range(HEADS_PER_GROUP - 2, -1, -1):
        out = jnp.where(head_ids == h, cols[h], out)
    return out


def _block_rows(j):
    return pl.ds(pl.multiple_of(j * ATTN_TILE, ATTN_TILE), ATTN_TILE)


def _sb_attn_kernel(q_ref, k_ref, v_ref, o_ref, ws_ref, cs_ref, acc_ref):
    t = ATTN_TILE
    i = pl.program_id(2)
    head_ids = _head_ids(t)
    q_heads = _masked_queries(q_ref, head_ids)
    row = lax.broadcasted_iota(jnp.int32, (t, t), 0)
    col = lax.broadcasted_iota(jnp.int32, (t, t), 1)
    strict = col < row
    neg_suffix_ones = jnp.where(row >= col, -1.0, 0.0).astype(BF16)

    def weights(j, diagonal):
        cs = [cs_ref[h] for h in range(HEADS_PER_STEP)]
        ws = []
        for h, z in enumerate(_key_scores(q_heads, k_ref, _block_rows(j))):
            sp = jnp.maximum(z, jnp.log(1.0 + jnp.exp2(jnp.minimum(z, SOFTPLUS_CLAMP))) * LOG2_E)
            if diagonal:
                sp = jnp.where(strict, sp, 0.0)
            suffix = jnp.dot(sp.astype(BF16), neg_suffix_ones, preferred_element_type=F32)
            w = jnp.exp2(z + suffix + _lane_tiled(cs[h], t))
            if diagonal:
                w = jnp.where(strict, w, 0.0)
            ws.append(w.astype(BF16))
            cs[h] = cs[h] - jnp.sum(sp, axis=1, keepdims=True)
        for h in range(HEADS_PER_STEP):
            ws_ref[h] = ws[h]
            cs_ref[h] = cs[h]

    def add_pv(j):
        for g, pv in enumerate(_weighted_values(ws_ref, v_ref, _block_rows(j), head_ids)):
            acc_ref[g] += pv

    def trip(n, carry):
        j = i - 1 - n
        add_pv(j + 1)
        weights(j, False)
        return carry

    cs_ref[...] = jnp.zeros(cs_ref.shape, F32)
    acc_ref[...] = jnp.zeros(acc_ref.shape, F32)
    weights(i, True)
    lax.fori_loop(0, i, trip, 0)
    add_pv(0)
    for g in range(GROUPS_PER_STEP):
        o_ref[0, :, _group_lanes(g)] = acc_ref[g].astype(o_ref.dtype)


def _sb_attention(qkv, batch, seq):
    t = ATTN_TILE
    k_off = D_MODEL // STEP_LANES
    return pl.pallas_call(
        _sb_attn_kernel,
        grid=(batch, HEAD_STEPS, seq // t),
        in_specs=[
            pl.BlockSpec((1, t, STEP_LANES), lambda b, p, i: (b, i, p)),
            pl.BlockSpec((1, seq, STEP_LANES), lambda b, p, i: (b, 0, k_off + p)),
            pl.BlockSpec((1, seq, STEP_LANES), lambda b, p, i: (b, 0, 2 * k_off + p)),
        ],
        out_specs=pl.BlockSpec((1, t, STEP_LANES), lambda b, p, i: (b, i, p)),
        out_shape=jax.ShapeDtypeStruct((batch, seq, D_MODEL), BF16),
        scratch_shapes=_attn_scratch(n_stats=1),
        compiler_params=_params("parallel", "parallel", "arbitrary"),
        name="sb_attention",
    )(qkv, qkv, qkv)


def _forget_gate_kernel(h_ref, g_ref, wf_ref, bf_ref, o_ref):
    t = ATTN_TILE
    seq = h_ref.shape[1]
    u = _rms_normed(h_ref[0], g_ref[...]).astype(BF16)
    logits = lax.dot_general(wf_ref[...], u, NT_DIMS, preferred_element_type=F32) + bf_ref[...]
    log_f = jnp.minimum(logits, 0.0) - jnp.log(1.0 + jnp.exp(-jnp.abs(logits)))
    row = lax.broadcasted_iota(jnp.int32, (t, t), 0)
    col = lax.broadcasted_iota(jnp.int32, (t, t), 1)
    prefix_ones = (row <= col).astype(BF16)
    carry = jnp.zeros((N_HEADS, 1), F32)
    for n in range(seq // t):
        blk = log_f[:, n * t:(n + 1) * t]
        hi = blk.astype(BF16)
        lo = (blk - hi.astype(F32)).astype(BF16)
        cs = (jnp.dot(hi, prefix_ones, preferred_element_type=F32)
              + jnp.dot(lo, prefix_ones, preferred_element_type=F32) + carry)
        o_ref[0, :, n * t:(n + 1) * t] = cs * LOG2_E
        carry = cs[:, t - 1:t]


def _forget_gates(h3, gain, w_f_t, b_f):
    batch, seq, d = h3.shape
    return pl.pallas_call(
        _forget_gate_kernel,
        grid=(batch,),
        in_specs=[
            pl.BlockSpec((1, seq, d), lambda b: (b, 0, 0)),
            pl.BlockSpec((1, d), lambda b: (0, 0)),
            pl.BlockSpec((N_HEADS, d), lambda b: (0, 0)),
            pl.BlockSpec((N_HEADS, 1), lambda b: (0, 0)),
        ],
        out_specs=pl.BlockSpec((1, N_HEADS, seq), lambda b: (b, 0, 0)),
        out_shape=jax.ShapeDtypeStruct((batch, N_HEADS, seq), F32),
        compiler_params=_params("parallel"),
        name="forget_gates",
    )(h3, gain.reshape(1, d), w_f_t, b_f.reshape(N_HEADS, 1))


def _fox_attn_kernel(q_ref, k_ref, v_ref, c_ref, o_ref, ps_ref, ms_ref, ls_ref, acc_ref):
    t = ATTN_TILE
    i = pl.program_id(2)
    head_ids = _head_ids(t)
    q_heads = _masked_queries(q_ref, head_ids)
    row = lax.broadcasted_iota(jnp.int32, (t, t), 0)
    col = lax.broadcasted_iota(jnp.int32, (t, t), 1)
    causal = col <= row

    def probs(j, diagonal):
        cb = c_ref[0, 0, j]
        ms = [ms_ref[h] for h in range(HEADS_PER_STEP)]
        ls = [ls_ref[h] for h in range(HEADS_PER_STEP)]
        ps, alphas = [], []
        for h, qk in enumerate(_key_scores(q_heads, k_ref, _block_rows(j))):
            s = qk - cb[h:h + 1, :]
            if diagonal:
                s = jnp.where(causal, s, -jnp.inf)
            m_new = jnp.maximum(ms[h], jnp.max(s, axis=1, keepdims=True))
            p = jnp.exp2(s - _lane_tiled(m_new, t))
            alpha = jnp.exp2(ms[h] - m_new)
            ps.append(p.astype(BF16))
            ms[h] = m_new
            ls[h] = alpha * ls[h] + jnp.sum(p, axis=1, keepdims=True)
            alphas.append(alpha)
        for h in range(HEADS_PER_STEP):
            ps_ref[h] = ps[h]
            ms_ref[h] = ms[h]
            ls_ref[h] = ls[h]
        return alphas

    def group_lanes_of(cols):
        return [_per_head_lanes(head_ids, cols[g * HEADS_PER_GROUP:(g + 1) * HEADS_PER_GROUP])
                for g in range(GROUPS_PER_STEP)]

    def trip(n, carry):
        j = i - 1 - n
        pvs = _weighted_values(ps_ref, v_ref, _block_rows(j + 1), head_ids)
        alphas = probs(j, False)
        for g, a in enumerate(group_lanes_of(alphas)):
            acc_ref[g] = (acc_ref[g] + pvs[g]) * a
        return carry

    ms_ref[...] = jnp.full(ms_ref.shape, -jnp.inf, F32)
    ls_ref[...] = jnp.zeros(ls_ref.shape, F32)
    acc_ref[...] = jnp.zeros(acc_ref.shape, F32)
    probs(i, True)
    lax.fori_loop(0, i, trip, 0)
    pvs = _weighted_values(ps_ref, v_ref, _block_rows(0), head_ids)
    denoms = group_lanes_of([ls_ref[h] for h in range(HEADS_PER_STEP)])
    for g in range(GROUPS_PER_STEP):
        o_ref[0, :, _group_lanes(g)] = ((acc_ref[g] + pvs[g]) / denoms[g]).astype(o_ref.dtype)


def _fox_attention(q, kv, c_blocks, batch, seq):
    t = ATTN_TILE
    v_off = D_MODEL // STEP_LANES
    return pl.pallas_call(
        _fox_attn_kernel,
        grid=(batch, HEAD_STEPS, seq // t),
        in_specs=[
            pl.BlockSpec((1, t, STEP_LANES), lambda b, p, i: (b, i, p)),
            pl.BlockSpec((1, seq, STEP_LANES), lambda b, p, i: (b, 0, p)),
            pl.BlockSpec((1, seq, STEP_LANES), lambda b, p, i: (b, 0, v_off + p)),
            pl.BlockSpec((1, 1, seq // t, HEADS_PER_STEP, t), lambda b, p, i: (b, p, 0, 0, 0)),
        ],
        out_specs=pl.BlockSpec((1, t, STEP_LANES), lambda b, p, i: (b, i, p)),
        out_shape=jax.ShapeDtypeStruct((batch, seq, D_MODEL), BF16),
        scratch_shapes=_attn_scratch(n_stats=2),
        compiler_params=_params("parallel", "parallel", "arbitrary"),
        name="fox_attention",
    )(q, kv, kv, c_blocks)


def _first_index_of_max(x, lane, valid):
    masked = jnp.where(valid, x, -jnp.inf)
    top = jnp.max(masked, axis=1, keepdims=True)
    idx = jnp.min(jnp.where(masked == top, lane, LANES), axis=1, keepdims=True)
    return top, idx


def _router_kernel(h_ref, g_ref, whi_ref, wlo_ref, xn_ref, cmb_ref):
    xn = _rms_normed(h_ref[...], g_ref[...])
    x_hi = xn.astype(BF16)
    xn_ref[...] = x_hi
    x_lo = (xn - x_hi.astype(F32)).astype(BF16)
    logits = (jnp.dot(x_hi, whi_ref[...], preferred_element_type=F32)
              + jnp.dot(x_lo, whi_ref[...], preferred_element_type=F32)
              + jnp.dot(x_hi, wlo_ref[...], preferred_element_type=F32))
    lane = lax.broadcasted_iota(jnp.int32, logits.shape, 1)
    is_group = (lane >= N_EXPERTS) & (lane < N_EXPERTS + N_GROUPS)
    g_top, g_lane = _first_index_of_max(logits, lane, is_group)
    g_sum = jnp.sum(jnp.where(is_group, jnp.exp(logits - g_top), 0.0), axis=1, keepdims=True)
    g_prob = 1.0 / g_sum
    g_idx = g_lane - N_EXPERTS
    in_group = (lane >= g_idx * EXPERTS_PER_GROUP) & (lane < (g_idx + 1) * EXPERTS_PER_GROUP)
    t1, i1 = _first_index_of_max(logits, lane, in_group)
    t2, i2 = _first_index_of_max(logits, lane, in_group & (lane != i1))
    e2 = jnp.exp(t2 - t1)
    w1 = g_prob / (1.0 + e2)
    w2 = g_prob * e2 / (1.0 + e2)
    cmb_ref[...] = jnp.where(lane == i1, w1, 0.0) + jnp.where(lane == i2, w2, 0.0)


def _router(h, gain, w_hi, w_lo):
    n, d = h.shape
    return pl.pallas_call(
        _router_kernel,
        grid=(n // ROW_TILE,),
        in_specs=[
            pl.BlockSpec((ROW_TILE, d), lambda i: (i, 0)),
            pl.BlockSpec((1, d), lambda i: (0, 0)),
            pl.BlockSpec((d, LANES), lambda i: (0, 0)),
            pl.BlockSpec((d, LANES), lambda i: (0, 0)),
        ],
        out_specs=[
            pl.BlockSpec((ROW_TILE, d), lambda i: (i, 0)),
            pl.BlockSpec((ROW_TILE, LANES), lambda i: (i, 0)),
        ],
        out_shape=[
            jax.ShapeDtypeStruct((n, d), BF16),
            jax.ShapeDtypeStruct((n, LANES), F32),
        ],
        compiler_params=_params("parallel"),
        name="moe_router",
    )(h, gain.reshape(1, d), w_hi, w_lo)


def _moe_dense_kernel(xn_ref, cmb_ref, h_ref, wg_ref, wu_ref, wd_ref, o_ref):
    e = pl.program_id(1)

    @pl.when(e == 0)
    def _():
        o_ref[...] = h_ref[...]

    x = xn_ref[...]
    gate = jnp.dot(x, wg_ref[0], preferred_element_type=F32)
    up = jnp.dot(x, wu_ref[0], preferred_element_type=F32)
    cmb = cmb_ref[...]
    lane = lax.broadcasted_iota(jnp.int32, cmb.shape, 1)
    scale = jnp.sum(jnp.where(lane == e, cmb, 0.0), axis=1, keepdims=True)
    act = gate / (1.0 + jnp.exp(-gate)) * up * scale
    o_ref[...] += jnp.dot(act.astype(BF16), wd_ref[0], preferred_element_type=F32)


def _moe_dense(xn, cmb, h, w_gate, w_up, w_down):
    n, d = h.shape
    f = w_gate.shape[2]
    rows = MOE_ROW_TILE
    return pl.pallas_call(
        _moe_dense_kernel,
        grid=(n // rows, N_EXPERTS),
        in_specs=[
            pl.BlockSpec((rows, d), lambda i, e: (i, 0)),
            pl.BlockSpec((rows, LANES), lambda i, e: (i, 0)),
            pl.BlockSpec((rows, d), lambda i, e: (i, 0)),
            pl.BlockSpec((1, d, f), lambda i, e: (e, 0, 0)),
            pl.BlockSpec((1, d, f), lambda i, e: (e, 0, 0)),
            pl.BlockSpec((1, f, d), lambda i, e: (e, 0, 0)),
        ],
        out_specs=pl.BlockSpec((rows, d), lambda i, e: (i, 0)),
        out_shape=jax.ShapeDtypeStruct((n, d), F32),
        compiler_params=_params("parallel", "arbitrary"),
        name="moe_experts",
    )(xn, cmb, h, w_gate, w_up, w_down)


def _rms_norm_kernel(h_ref, g_ref, o_ref):
    o_ref[...] = _rms_normed(h_ref[...], g_ref[...])


def _final_norm(h, gain):
    n, d = h.shape
    return pl.pallas_call(
        _rms_norm_kernel,
        grid=(n // ROW_TILE,),
        in_specs=[
            pl.BlockSpec((ROW_TILE, d), lambda i: (i, 0)),
            pl.BlockSpec((1, d), lambda i: (0, 0)),
        ],
        out_specs=pl.BlockSpec((ROW_TILE, d), lambda i: (i, 0)),
        out_shape=jax.ShapeDtypeStruct((n, d), F32),
        compiler_params=_params("parallel"),
        name="final_norm",
    )(h, gain.reshape(1, d))


def _moe_layer(h, gain, w_group, w_router, w_gate, w_up, w_down):
    d = h.shape[1]
    w_r = jnp.zeros((d, LANES), F32)
    w_r = w_r.at[:, :N_EXPERTS].set(w_router).at[:, N_EXPERTS:N_EXPERTS + N_GROUPS].set(w_group)
    w_hi = w_r.astype(BF16)
    w_lo = (w_r - w_hi.astype(F32)).astype(BF16)
    xn, cmb = _router(h, gain, w_hi, w_lo)
    return _moe_dense(xn, cmb, h, w_gate.astype(BF16), w_up.astype(BF16), w_down.astype(BF16))


def kernel(x, attn_norm, w_qkv_a, w_o_a, kv_norm, w_kvf, b_f, w_q_b, w_o_b, moe_norm,
           w_group, w_router, w_gate, w_up, w_down, final_norm):
    batch, seq, d = x.shape
    n = batch * seq
    q_scale = HEAD_DIM ** -0.5 * LOG2_E
    h = x.reshape(n, d)
    n_a = w_qkv_a.shape[0]
    depth = attn_norm.shape[0]
    kv = c_blocks = None
    for layer in range(depth):
        if layer < n_a:
            w = w_qkv_a[layer]
            w = jnp.concatenate([w[:, :d] * q_scale, w[:, d:]], axis=1).astype(BF16)
            qkv = _norm_matmul(h, attn_norm[layer], w, 1024)
            mix = _sb_attention(qkv.reshape(batch, seq, 3 * d), batch, seq)
            h = _matmul_residual(mix.reshape(n, d), w_o_a[layer].astype(BF16), h)
        else:
            j = layer - n_a
            if j == 0:
                kv = _norm_matmul(h, kv_norm, w_kvf[:, :2 * d].astype(BF16), 1024)
                w_f_t = w_kvf[:, 2 * d:].T.astype(BF16)
                cum = _forget_gates(h.reshape(batch, seq, d), kv_norm, w_f_t, b_f)
                t = ATTN_TILE
                c_blocks = cum.reshape(batch, HEAD_STEPS, HEADS_PER_STEP, seq // t, t).transpose(0, 1, 3, 2, 4)
                kv = kv.reshape(batch, seq, 2 * d)
            q = _norm_matmul(h, attn_norm[layer], (w_q_b[j] * q_scale).astype(BF16), 1024)
            mix = _fox_attention(q.reshape(batch, seq, d), kv, c_blocks, batch, seq)
            h = _matmul_residual(mix.reshape(n, d), w_o_b[j].astype(BF16), h)
        h = _moe_layer(h, moe_norm[layer], w_group[layer], w_router[layer],
                       w_gate[layer], w_up[layer], w_down[layer])
    return _final_norm(h, final_norm).reshape(batch, seq, d)
```

```python
import functools

import jax
import jax.numpy as jnp
from jax import lax
from jax.experimental import pallas as pl
from jax.experimental.pallas import tpu as pltpu

D_MODEL = 1024
N_HEADS = 16
HEAD_DIM = 64
N_GROUPS = 4
EXPERTS_PER_GROUP = 4
N_EXPERTS = 16
D_EXPERT = 512
RMS_EPS = 1e-6

LANES = 128
MXU_LANES = 256
HEADS_PER_GROUP = MXU_LANES // HEAD_DIM
GROUPS_PER_STEP = 1
HEADS_PER_STEP = HEADS_PER_GROUP * GROUPS_PER_STEP
STEP_LANES = MXU_LANES * GROUPS_PER_STEP
HEAD_STEPS = N_HEADS // HEADS_PER_STEP
LOG2_E = 1.4426950408889634
SOFTPLUS_CLAMP = 64.0
ROW_TILE = 512
MOE_ROW_TILE = 1024
ATTN_TILE = 256
VMEM_LIMIT = 48 * 1024 * 1024
MOE_VMEM_LIMIT = 56 * 1024 * 1024

F32 = jnp.float32
BF16 = jnp.bfloat16
NT_DIMS = (((1,), (1,)), ((), ()))


def _params(*semantics):
    return pltpu.CompilerParams(dimension_semantics=semantics, vmem_limit_bytes=VMEM_LIMIT)


def _rms_normed(x, gain):
    var = jnp.mean(x * x, axis=-1, keepdims=True)
    return x * lax.rsqrt(var + RMS_EPS) * gain


def _norm_matmul_kernel(h_ref, g_ref, w_ref, o_ref):
    xn = _rms_normed(h_ref[...], g_ref[...]).astype(BF16)
    o_ref[...] = jnp.dot(xn, w_ref[...], preferred_element_type=F32).astype(o_ref.dtype)


def _norm_matmul(h, gain, w):
    n, d = h.shape
    n_out = w.shape[1]
    return pl.pallas_call(
        _norm_matmul_kernel,
        grid=(n // ROW_TILE,),
        in_specs=[
            pl.BlockSpec((ROW_TILE, d), lambda i: (i, 0)),
            pl.BlockSpec((1, d), lambda i: (0, 0)),
            pl.BlockSpec((d, n_out), lambda i: (0, 0)),
        ],
        out_specs=pl.BlockSpec((ROW_TILE, n_out), lambda i: (i, 0)),
        out_shape=jax.ShapeDtypeStruct((n, n_out), BF16),
        compiler_params=_params("parallel"),
        name="norm_matmul",
    )(h, gain.reshape(1, d), w)


def _matmul_residual_kernel(mix_ref, w_ref, h_ref, o_ref):
    o_ref[...] = h_ref[...] + jnp.dot(mix_ref[...], w_ref[...], preferred_element_type=F32)


def _matmul_residual(mix, w, h):
    n, d = h.shape
    return pl.pallas_call(
        _matmul_residual_kernel,
        grid=(n // ROW_TILE,),
        in_specs=[
            pl.BlockSpec((ROW_TILE, d), lambda i: (i, 0)),
            pl.BlockSpec((d, d), lambda i: (0, 0)),
            pl.BlockSpec((ROW_TILE, d), lambda i: (i, 0)),
        ],
        out_specs=pl.BlockSpec((ROW_TILE, d), lambda i: (i, 0)),
        out_shape=jax.ShapeDtypeStruct((n, d), F32),
        compiler_params=_params("parallel"),
        name="matmul_residual",
    )(mix, w, h)


def _head_ids(rows):
    lane = lax.broadcasted_iota(jnp.int32, (rows, MXU_LANES), 1)
    return lane // HEAD_DIM


def _only_head(x, head_ids, h):
    return jnp.where(head_ids == h, x, jnp.zeros_like(x))


def _group_lanes(g):
    return slice(g * MXU_LANES, (g + 1) * MXU_LANES)


def _masked_queries(q_ref, head_ids):
    return [_only_head(q_ref[0, :, _group_lanes(g)], head_ids, h)
            for g in range(GROUPS_PER_STEP) for h in range(HEADS_PER_GROUP)]


def _key_scores(q_heads, k_ref, rows):
    out = []
    for g in range(GROUPS_PER_STEP):
        kb = k_ref[0, rows, _group_lanes(g)]
        for h in range(HEADS_PER_GROUP):
            out.append(lax.dot_general(q_heads[g * HEADS_PER_GROUP + h], kb, NT_DIMS,
                                       preferred_element_type=F32))
    return out


def _weighted_values(ws_ref, v_ref, rows, head_ids):
    out = []
    for g in range(GROUPS_PER_STEP):
        vb = v_ref[0, rows, _group_lanes(g)]
        acc = None
        for h in range(HEADS_PER_GROUP):
            pv = jnp.dot(ws_ref[g * HEADS_PER_GROUP + h], _only_head(vb, head_ids, h),
                         preferred_element_type=F32)
            acc = pv if acc is None else acc + pv
        out.append(acc)
    return out


def _attn_scratch(n_stats):
    t = ATTN_TILE
    return ([pltpu.VMEM((HEADS_PER_STEP, t, t), BF16)]
            + [pltpu.VMEM((HEADS_PER_STEP, t, LANES), F32)] * n_stats
            + [pltpu.VMEM((GROUPS_PER_STEP, t, MXU_LANES), F32)])


def _lane_tiled(x, width):
    return jnp.tile(x, (1, width // LANES))


def _per_head_lanes(head_ids, cols):
    cols = [_lane_tiled(c, MXU_LANES) for c in cols]
    out = cols[-1]
    for h in range(HEADS_PER_GROUP - 2, -1, -1):
        out = jnp.where(head_ids == h, cols[h], out)
    return out


def _block_rows(j):
    return pl.ds(pl.multiple_of(j * ATTN_TILE, ATTN_TILE), ATTN_TILE)


def _sb_attn_kernel(q_ref, k_ref, v_ref, o_ref, ws_ref, cs_ref, acc_ref):
    t = ATTN_TILE
    i = pl.program_id(2)
    head_ids = _head_ids(t)
    q_heads = _masked_queries(q_ref, head_ids)
    row = lax.broadcasted_iota(jnp.int32, (t, t), 0)
    col = lax.broadcasted_iota(jnp.int32, (t, t), 1)
    strict = col < row
    neg_suffix_ones = jnp.where(row >= col, -1.0, 0.0).astype(BF16)

    def weights(j, diagonal):
        cs = [cs_ref[h] for h in range(HEADS_PER_STEP)]
        ws = []
        for h, z in enumerate(_key_scores(q_heads, k_ref, _block_rows(j))):
            sp = jnp.maximum(z, jnp.log(1.0 + jnp.exp2(jnp.minimum(z, SOFTPLUS_CLAMP))) * LOG2_E)
            if diagonal:
                sp = jnp.where(strict, sp, 0.0)
            suffix = jnp.dot(sp.astype(BF16), neg_suffix_ones, preferred_element_type=F32)
            w = jnp.exp2(z + suffix + _lane_tiled(cs[h], t))
            if diagonal:
                w = jnp.where(strict, w, 0.0)
            ws.append(w.astype(BF16))
            cs[h] = cs[h] - jnp.sum(sp, axis=1, keepdims=True)
        for h in range(HEADS_PER_STEP):
            ws_ref[h] = ws[h]
            cs_ref[h] = cs[h]

    def add_pv(j):
        for g, pv in enumerate(_weighted_values(ws_ref, v_ref, _block_rows(j), head_ids)):
            acc_ref[g] += pv

    def trip(n, carry):
        j = i - 1 - n
        add_pv(j + 1)
        weights(j, False)
        return carry

    cs_ref[...] = jnp.zeros(cs_ref.shape, F32)
    acc_ref[...] = jnp.zeros(acc_ref.shape, F32)
    weights(i, True)
    lax.fori_loop(0, i, trip, 0)
    add_pv(0)
    for g in range(GROUPS_PER_STEP):
        o_ref[0, :, _group_lanes(g)] = acc_ref[g].astype(o_ref.dtype)


def _sb_attention(qkv, batch, seq):
    t = ATTN_TILE
    k_off = D_MODEL // STEP_LANES
    return pl.pallas_call(
        _sb_attn_kernel,
        grid=(batch, HEAD_STEPS, seq // t),
        in_specs=[
            pl.BlockSpec((1, t, STEP_LANES), lambda b, p, i: (b, i, p)),
            pl.BlockSpec((1, seq, STEP_LANES), lambda b, p, i: (b, 0, k_off + p)),
            pl.BlockSpec((1, seq, STEP_LANES), lambda b, p, i: (b, 0, 2 * k_off + p)),
        ],
        out_specs=pl.BlockSpec((1, t, STEP_LANES), lambda b, p, i: (b, i, p)),
        out_shape=jax.ShapeDtypeStruct((batch, seq, D_MODEL), BF16),
        scratch_shapes=_attn_scratch(n_stats=1),
        compiler_params=_params("parallel", "parallel", "arbitrary"),
        name="sb_attention",
    )(qkv, qkv, qkv)


def _forget_gate_kernel(h_ref, g_ref, wf_ref, bf_ref, o_ref):
    t = ATTN_TILE
    seq = h_ref.shape[1]
    u = _rms_normed(h_ref[0], g_ref[...]).astype(BF16)
    logits = lax.dot_general(wf_ref[...], u, NT_DIMS, preferred_element_type=F32) + bf_ref[...]
    log_f = jnp.minimum(logits, 0.0) - jnp.log(1.0 + jnp.exp(-jnp.abs(logits)))
    row = lax.broadcasted_iota(jnp.int32, (t, t), 0)
    col = lax.broadcasted_iota(jnp.int32, (t, t), 1)
    prefix_ones = (row <= col).astype(BF16)
    carry = jnp.zeros((N_HEADS, 1), F32)
    for n in range(seq // t):
        blk = log_f[:, n * t:(n + 1) * t]
        hi = blk.astype(BF16)
        lo = (blk - hi.astype(F32)).astype(BF16)
        cs = (jnp.dot(hi, prefix_ones, preferred_element_type=F32)
              + jnp.dot(lo, prefix_ones, preferred_element_type=F32) + carry)
        o_ref[0, :, n * t:(n + 1) * t] = cs * LOG2_E
        carry = cs[:, t - 1:t]


def _forget_gates(h3, gain, w_f_t, b_f):
    batch, seq, d = h3.shape
    return pl.pallas_call(
        _forget_gate_kernel,
        grid=(batch,),
        in_specs=[
            pl.BlockSpec((1, seq, d), lambda b: (b, 0, 0)),
            pl.BlockSpec((1, d), lambda b: (0, 0)),
            pl.BlockSpec((N_HEADS, d), lambda b: (0, 0)),
            pl.BlockSpec((N_HEADS, 1), lambda b: (0, 0)),
        ],
        out_specs=pl.BlockSpec((1, N_HEADS, seq), lambda b: (b, 0, 0)),
        out_shape=jax.ShapeDtypeStruct((batch, N_HEADS, seq), F32),
        compiler_params=_params("parallel"),
        name="forget_gates",
    )(h3, gain.reshape(1, d), w_f_t, b_f.reshape(N_HEADS, 1))


def _fox_attn_kernel(q_ref, k_ref, v_ref, c_ref, o_ref, ps_ref, ms_ref, ls_ref, acc_ref):
    t = ATTN_TILE
    i = pl.program_id(2)
    head_ids = _head_ids(t)
    q_heads = _masked_queries(q_ref, head_ids)
    row = lax.broadcasted_iota(jnp.int32, (t, t), 0)
    col = lax.broadcasted_iota(jnp.int32, (t, t), 1)
    causal = col <= row

    def probs(j, diagonal):
        cb = c_ref[0, 0, j]
        ms = [ms_ref[h] for h in range(HEADS_PER_STEP)]
        ls = [ls_ref[h] for h in range(HEADS_PER_STEP)]
        ps, alphas = [], []
        for h, qk in enumerate(_key_scores(q_heads, k_ref, _block_rows(j))):
            s = qk - cb[h:h + 1, :]
            if diagonal:
                s = jnp.where(causal, s, -jnp.inf)
            m_new = jnp.maximum(ms[h], jnp.max(s, axis=1, keepdims=True))
            p = jnp.exp2(s - _lane_tiled(m_new, t))
            alpha = jnp.exp2(ms[h] - m_new)
            ps.append(p.astype(BF16))
            ms[h] = m_new
            ls[h] = alpha * ls[h] + jnp.sum(p, axis=1, keepdims=True)
            alphas.append(alpha)
        for h in range(HEADS_PER_STEP):
            ps_ref[h] = ps[h]
            ms_ref[h] = ms[h]
            ls_ref[h] = ls[h]
        return alphas

    def group_lanes_of(cols):
        return [_per_head_lanes(head_ids, cols[g * HEADS_PER_GROUP:(g + 1) * HEADS_PER_GROUP])
                for g in range(GROUPS_PER_STEP)]

    def trip(n, carry):
        j = i - 1 - n
        pvs = _weighted_values(ps_ref, v_ref, _block_rows(j + 1), head_ids)
        alphas = probs(j, False)
        for g, a in enumerate(group_lanes_of(alphas)):
            acc_ref[g] = (acc_ref[g] + pvs[g]) * a
        return carry

    ms_ref[...] = jnp.full(ms_ref.shape, -jnp.inf, F32)
    ls_ref[...] = jnp.zeros(ls_ref.shape, F32)
    acc_ref[...] = jnp.zeros(acc_ref.shape, F32)
    probs(i, True)
    lax.fori_loop(0, i, trip, 0)
    pvs = _weighted_values(ps_ref, v_ref, _block_rows(0), head_ids)
    denoms = group_lanes_of([ls_ref[h] for h in range(HEADS_PER_STEP)])
    for g in range(GROUPS_PER_STEP):
        o_ref[0, :, _group_lanes(g)] = ((acc_ref[g] + pvs[g]) / denoms[g]).astype(o_ref.dtype)


def _fox_attention(q, kv, c_blocks, batch, seq):
    t = ATTN_TILE
    v_off = D_MODEL // STEP_LANES
    return pl.pallas_call(
        _fox_attn_kernel,
        grid=(batch, HEAD_STEPS, seq // t),
        in_specs=[
            pl.BlockSpec((1, t, STEP_LANES), lambda b, p, i: (b, i, p)),
            pl.BlockSpec((1, seq, STEP_LANES), lambda b, p, i: (b, 0, p)),
            pl.BlockSpec((1, seq, STEP_LANES), lambda b, p, i: (b, 0, v_off + p)),
            pl.BlockSpec((1, 1, seq // t, HEADS_PER_STEP, t), lambda b, p, i: (b, p, 0, 0, 0)),
        ],
        out_specs=pl.BlockSpec((1, t, STEP_LANES), lambda b, p, i: (b, i, p)),
        out_shape=jax.ShapeDtypeStruct((batch, seq, D_MODEL), BF16),
        scratch_shapes=_attn_scratch(n_stats=2),
        compiler_params=_params("parallel", "parallel", "arbitrary"),
        name="fox_attention",
    )(q, kv, kv, c_blocks)


def _first_index_of_max(x, lane, valid):
    masked = jnp.where(valid, x, -jnp.inf)
    top = jnp.max(masked, axis=1, keepdims=True)
    idx = jnp.min(jnp.where(masked == top, lane, LANES), axis=1, keepdims=True)
    return top, idx


def _router_kernel(h_ref, g_ref, whi_ref, wlo_ref, xn_ref, cmb_ref):
    xn = _rms_normed(h_ref[...], g_ref[...])
    x_hi = xn.astype(BF16)
    xn_ref[...] = x_hi
    x_lo = (xn - x_hi.astype(F32)).astype(BF16)
    logits = (jnp.dot(x_hi, whi_ref[...], preferred_element_type=F32)
              + jnp.dot(x_lo, whi_ref[...], preferred_element_type=F32)
              + jnp.dot(x_hi, wlo_ref[...], preferred_element_type=F32))
    lane = lax.broadcasted_iota(jnp.int32, logits.shape, 1)
    is_group = (lane >= N_EXPERTS) & (lane < N_EXPERTS + N_GROUPS)
    g_top, g_lane = _first_index_of_max(logits, lane, is_group)
    g_sum = jnp.sum(jnp.where(is_group, jnp.exp(logits - g_top), 0.0), axis=1, keepdims=True)
    g_prob = 1.0 / g_sum
    g_idx = g_lane - N_EXPERTS
    in_group = (lane >= g_idx * EXPERTS_PER_GROUP) & (lane < (g_idx + 1) * EXPERTS_PER_GROUP)
    t1, i1 = _first_index_of_max(logits, lane, in_group)
    t2, i2 = _first_index_of_max(logits, lane, in_group & (lane != i1))
    e2 = jnp.exp(t2 - t1)
    w1 = g_prob / (1.0 + e2)
    w2 = g_prob * e2 / (1.0 + e2)
    cmb_ref[...] = jnp.where(lane == i1, w1, 0.0) + jnp.where(lane == i2, w2, 0.0)


def _router(h, gain, w_hi, w_lo):
    n, d = h.shape
    return pl.pallas_call(
        _router_kernel,
        grid=(n // ROW_TILE,),
        in_specs=[
            pl.BlockSpec((ROW_TILE, d), lambda i: (i, 0)),
            pl.BlockSpec((1, d), lambda i: (0, 0)),
            pl.BlockSpec((d, LANES), lambda i: (0, 0)),
            pl.BlockSpec((d, LANES), lambda i: (0, 0)),
        ],
        out_specs=[
            pl.BlockSpec((ROW_TILE, d), lambda i: (i, 0)),
            pl.BlockSpec((ROW_TILE, LANES), lambda i: (i, 0)),
        ],
        out_shape=[
            jax.ShapeDtypeStruct((n, d), BF16),
            jax.ShapeDtypeStruct((n, LANES), F32),
        ],
        compiler_params=_params("parallel"),
        name="moe_router",
    )(h, gain.reshape(1, d), w_hi, w_lo)


def _moe_dense_kernel(xn_ref, cmb_ref, h_ref, wg_ref, wu_ref, wd_ref, fg_ref, o_ref, *, final_norm):
    e = pl.program_id(1)

    @pl.when(e == 0)
    def _():
        o_ref[...] = h_ref[...]

    x = xn_ref[...]
    gate = jnp.dot(x, wg_ref[0, 0].astype(BF16), preferred_element_type=F32)
    up = jnp.dot(x, wu_ref[0, 0].astype(BF16), preferred_element_type=F32)
    cmb = cmb_ref[...]
    lane = lax.broadcasted_iota(jnp.int32, cmb.shape, 1)
    scale = jnp.sum(jnp.where(lane == e, cmb, 0.0), axis=1, keepdims=True)
    act = gate / (1.0 + jnp.exp(-gate)) * up * scale
    o_ref[...] += jnp.dot(act.astype(BF16), wd_ref[0, 0].astype(BF16), preferred_element_type=F32)

    if final_norm:
        @pl.when(e == pl.num_programs(1) - 1)
        def _():
            o_ref[...] = _rms_normed(o_ref[...], fg_ref[...])


def _moe_dense(xn, cmb, h, layer, w_gate, w_up, w_down, final_gain):
    n, d = h.shape
    f = w_gate.shape[3]
    rows = MOE_ROW_TILE
    final_norm = final_gain is not None
    gain = final_gain if final_norm else jnp.ones((d,), F32)
    return pl.pallas_call(
        functools.partial(_moe_dense_kernel, final_norm=final_norm),
        grid=(n // rows, N_EXPERTS),
        in_specs=[
            pl.BlockSpec((rows, d), lambda i, e: (i, 0)),
            pl.BlockSpec((rows, LANES), lambda i, e: (i, 0)),
            pl.BlockSpec((rows, d), lambda i, e: (i, 0)),
            pl.BlockSpec((1, 1, d, f), lambda i, e: (layer, e, 0, 0)),
            pl.BlockSpec((1, 1, d, f), lambda i, e: (layer, e, 0, 0)),
            pl.BlockSpec((1, 1, f, d), lambda i, e: (layer, e, 0, 0)),
            pl.BlockSpec((1, d), lambda i, e: (0, 0)),
        ],
        out_specs=pl.BlockSpec((rows, d), lambda i, e: (i, 0)),
        out_shape=jax.ShapeDtypeStruct((n, d), F32),
        compiler_params=pltpu.CompilerParams(dimension_semantics=("parallel", "arbitrary"),
                                             vmem_limit_bytes=MOE_VMEM_LIMIT),
        name="moe_experts",
    )(xn, cmb, h, w_gate, w_up, w_down, gain.reshape(1, d))


def _moe_layer(h, gain, w_group, w_router, layer, w_gate, w_up, w_down, final_gain):
    d = h.shape[1]
    w_r = jnp.zeros((d, LANES), F32)
    w_r = w_r.at[:, :N_EXPERTS].set(w_router).at[:, N_EXPERTS:N_EXPERTS + N_GROUPS].set(w_group)
    w_hi = w_r.astype(BF16)
    w_lo = (w_r - w_hi.astype(F32)).astype(BF16)
    xn, cmb = _router(h, gain, w_hi, w_lo)
    return _moe_dense(xn, cmb, h, layer, w_gate, w_up, w_down, final_gain)


def kernel(x, attn_norm, w_qkv_a, w_o_a, kv_norm, w_kvf, b_f, w_q_b, w_o_b, moe_norm,
           w_group, w_router, w_gate, w_up, w_down, final_norm):
    batch, seq, d = x.shape
    n = batch * seq
    q_scale = HEAD_DIM ** -0.5 * LOG2_E
    h = x.reshape(n, d)
    n_a = w_qkv_a.shape[0]
    depth = attn_norm.shape[0]
    kv = c_blocks = None
    for layer in range(depth):
        if layer < n_a:
            w = w_qkv_a[layer]
            w = jnp.concatenate([w[:, :d] * q_scale, w[:, d:]], axis=1).astype(BF16)
            qkv = _norm_matmul(h, attn_norm[layer], w)
            mix = _sb_attention(qkv.reshape(batch, seq, 3 * d), batch, seq)
            h = _matmul_residual(mix.reshape(n, d), w_o_a[layer].astype(BF16), h)
        else:
            j = layer - n_a
            if j == 0:
                kv = _norm_matmul(h, kv_norm, w_kvf[:, :2 * d].astype(BF16))
                w_f_t = w_kvf[:, 2 * d:].T.astype(BF16)
                cum = _forget_gates(h.reshape(batch, seq, d), kv_norm, w_f_t, b_f)
                t = ATTN_TILE
                c_blocks = cum.reshape(batch, HEAD_STEPS, HEADS_PER_STEP, seq // t, t).transpose(0, 1, 3, 2, 4)
                kv = kv.reshape(batch, seq, 2 * d)
            q = _norm_matmul(h, attn_norm[layer], (w_q_b[j] * q_scale).astype(BF16))
            mix = _fox_attention(q.reshape(batch, seq, d), kv, c_blocks, batch, seq)
            h = _matmul_residual(mix.reshape(n, d), w_o_b[j].astype(BF16), h)
        final_gain = final_norm if layer == depth - 1 else None
        h = _moe_layer(h, moe_norm[layer], w_group[layer], w_router[layer], layer,
                       w_gate, w_up, w_down, final_gain)
    return h.reshape(batch, seq, d)
```

```python
import functools

import jax
import jax.numpy as jnp
from jax import lax
from jax.experimental import pallas as pl
from jax.experimental.pallas import tpu as pltpu

D_MODEL = 1024
N_HEADS = 16
HEAD_DIM = 64
N_GROUPS = 4
EXPERTS_PER_GROUP = 4
N_EXPERTS = 16
D_EXPERT = 512
RMS_EPS = 1e-6

LANES = 128
MXU_LANES = 256
HEADS_PER_GROUP = MXU_LANES // HEAD_DIM
SB_GROUPS = 1
FOX_GROUPS = 2
LOG2_E = 1.4426950408889634
SOFTPLUS_CLAMP = 64.0
ROW_TILE = 512
MOE_ROW_TILE = 1024
ATTN_TILE = 256
VMEM_LIMIT = 48 * 1024 * 1024
MOE_VMEM_LIMIT = 56 * 1024 * 1024

F32 = jnp.float32
BF16 = jnp.bfloat16
NT_DIMS = (((1,), (1,)), ((), ()))


def _params(*semantics):
    return pltpu.CompilerParams(dimension_semantics=semantics, vmem_limit_bytes=VMEM_LIMIT)


def _rms_normed(x, gain):
    var = jnp.mean(x * x, axis=-1, keepdims=True)
    return x * lax.rsqrt(var + RMS_EPS) * gain


def _norm_matmul_kernel(h_ref, g_ref, w_ref, o_ref):
    xn = _rms_normed(h_ref[...], g_ref[...]).astype(BF16)
    o_ref[...] = jnp.dot(xn, w_ref[...], preferred_element_type=F32).astype(o_ref.dtype)


def _norm_matmul(h, gain, w):
    n, d = h.shape
    n_out = w.shape[1]
    return pl.pallas_call(
        _norm_matmul_kernel,
        grid=(n // ROW_TILE,),
        in_specs=[
            pl.BlockSpec((ROW_TILE, d), lambda i: (i, 0)),
            pl.BlockSpec((1, d), lambda i: (0, 0)),
            pl.BlockSpec((d, n_out), lambda i: (0, 0)),
        ],
        out_specs=pl.BlockSpec((ROW_TILE, n_out), lambda i: (i, 0)),
        out_shape=jax.ShapeDtypeStruct((n, n_out), BF16),
        compiler_params=_params("parallel"),
        name="norm_matmul",
    )(h, gain.reshape(1, d), w)


def _matmul_residual_kernel(mix_ref, w_ref, h_ref, o_ref):
    o_ref[...] = h_ref[...] + jnp.dot(mix_ref[...], w_ref[...], preferred_element_type=F32)


def _matmul_residual(mix, w, h):
    n, d = h.shape
    return pl.pallas_call(
        _matmul_residual_kernel,
        grid=(n // ROW_TILE,),
        in_specs=[
            pl.BlockSpec((ROW_TILE, d), lambda i: (i, 0)),
            pl.BlockSpec((d, d), lambda i: (0, 0)),
            pl.BlockSpec((ROW_TILE, d), lambda i: (i, 0)),
        ],
        out_specs=pl.BlockSpec((ROW_TILE, d), lambda i: (i, 0)),
        out_shape=jax.ShapeDtypeStruct((n, d), F32),
        compiler_params=_params("parallel"),
        name="matmul_residual",
    )(mix, w, h)


def _head_ids(rows):
    lane = lax.broadcasted_iota(jnp.int32, (rows, MXU_LANES), 1)
    return lane // HEAD_DIM


def _only_head(x, head_ids, h):
    return jnp.where(head_ids == h, x, jnp.zeros_like(x))


def _group_lanes(g):
    return slice(g * MXU_LANES, (g + 1) * MXU_LANES)


def _masked_queries(q_ref, head_ids, groups):
    return [_only_head(q_ref[0, :, _group_lanes(g)], head_ids, h)
            for g in range(groups) for h in range(HEADS_PER_GROUP)]


def _key_scores(q_heads, k_ref, rows):
    out = []
    for g in range(len(q_heads) // HEADS_PER_GROUP):
        kb = k_ref[0, rows, _group_lanes(g)]
        for h in range(HEADS_PER_GROUP):
            out.append(lax.dot_general(q_heads[g * HEADS_PER_GROUP + h], kb, NT_DIMS,
                                       preferred_element_type=F32))
    return out


def _weighted_values(ws_ref, v_ref, rows, head_ids):
    out = []
    for g in range(ws_ref.shape[0] // HEADS_PER_GROUP):
        vb = v_ref[0, rows, _group_lanes(g)]
        acc = None
        for h in range(HEADS_PER_GROUP):
            pv = jnp.dot(ws_ref[g * HEADS_PER_GROUP + h], _only_head(vb, head_ids, h),
                         preferred_element_type=F32)
            acc = pv if acc is None else acc + pv
        out.append(acc)
    return out


def _attn_scratch(n_stats, groups):
    t = ATTN_TILE
    heads = groups * HEADS_PER_GROUP
    return ([pltpu.VMEM((heads, t, t), BF16)]
            + [pltpu.VMEM((heads, t, LANES), F32)] * n_stats
            + [pltpu.VMEM((groups, t, MXU_LANES), F32)])


def _lane_tiled(x, width):
    return jnp.tile(x, (1, width // LANES))


def _per_head_lanes(head_ids, cols):
    cols = [_lane_tiled(c, MXU_LANES) for c in cols]
    out = cols[-1]
    for h in range(HEADS_PER_GROUP - 2, -1, -1):
        out = jnp.where(head_ids == h, cols[h], out)
    return out


def _block_rows(j):
    return pl.ds(pl.multiple_of(j * ATTN_TILE, ATTN_TILE), ATTN_TILE)


def _sb_attn_kernel(q_ref, k_ref, v_ref, o_ref, ws_ref, cs_ref, acc_ref):
    t = ATTN_TILE
    heads, groups = ws_ref.shape[0], acc_ref.shape[0]
    i = pl.program_id(2)
    head_ids = _head_ids(t)
    q_heads = _masked_queries(q_ref, head_ids, groups)
    row = lax.broadcasted_iota(jnp.int32, (t, t), 0)
    col = lax.broadcasted_iota(jnp.int32, (t, t), 1)
    strict = col < row
    neg_suffix_ones = jnp.where(row >= col, -1.0, 0.0).astype(BF16)

    def weights(j, diagonal):
        cs = [cs_ref[h] for h in range(heads)]
        ws = []
        for h, z in enumerate(_key_scores(q_heads, k_ref, _block_rows(j))):
            sp = jnp.maximum(z, jnp.log(1.0 + jnp.exp2(jnp.minimum(z, SOFTPLUS_CLAMP))) * LOG2_E)
            if diagonal:
                sp = jnp.where(strict, sp, 0.0)
            suffix = jnp.dot(sp.astype(BF16), neg_suffix_ones, preferred_element_type=F32)
            w = jnp.exp2(z + suffix + _lane_tiled(cs[h], t))
            if diagonal:
                w = jnp.where(strict, w, 0.0)
            ws.append(w.astype(BF16))
            cs[h] = cs[h] - jnp.sum(sp, axis=1, keepdims=True)
        for h in range(heads):
            ws_ref[h] = ws[h]
            cs_ref[h] = cs[h]

    def add_pv(j):
        for g, pv in enumerate(_weighted_values(ws_ref, v_ref, _block_rows(j), head_ids)):
            acc_ref[g] += pv

    def trip(n, carry):
        j = i - 1 - n
        add_pv(j + 1)
        weights(j, False)
        return carry

    cs_ref[...] = jnp.zeros(cs_ref.shape, F32)
    acc_ref[...] = jnp.zeros(acc_ref.shape, F32)
    weights(i, True)
    lax.fori_loop(0, i, trip, 0)
    add_pv(0)
    for g in range(groups):
        o_ref[0, :, _group_lanes(g)] = acc_ref[g].astype(o_ref.dtype)


def _sb_attention(qkv, batch, seq):
    t = ATTN_TILE
    step_lanes = SB_GROUPS * MXU_LANES
    k_off = D_MODEL // step_lanes
    return pl.pallas_call(
        _sb_attn_kernel,
        grid=(batch, D_MODEL // step_lanes, seq // t),
        in_specs=[
            pl.BlockSpec((1, t, step_lanes), lambda b, p, i: (b, i, p)),
            pl.BlockSpec((1, seq, step_lanes), lambda b, p, i: (b, 0, k_off + p)),
            pl.BlockSpec((1, seq, step_lanes), lambda b, p, i: (b, 0, 2 * k_off + p)),
        ],
        out_specs=pl.BlockSpec((1, t, step_lanes), lambda b, p, i: (b, i, p)),
        out_shape=jax.ShapeDtypeStruct((batch, seq, D_MODEL), BF16),
        scratch_shapes=_attn_scratch(n_stats=1, groups=SB_GROUPS),
        compiler_params=_params("parallel", "parallel", "arbitrary"),
        name="sb_attention",
    )(qkv, qkv, qkv)


def _forget_gate_kernel(h_ref, g_ref, wf_ref, bf_ref, o_ref):
    t = ATTN_TILE
    seq = h_ref.shape[1]
    u = _rms_normed(h_ref[0], g_ref[...]).astype(BF16)
    logits = lax.dot_general(wf_ref[...], u, NT_DIMS, preferred_element_type=F32) + bf_ref[...]
    log_f = jnp.minimum(logits, 0.0) - jnp.log(1.0 + jnp.exp(-jnp.abs(logits)))
    row = lax.broadcasted_iota(jnp.int32, (t, t), 0)
    col = lax.broadcasted_iota(jnp.int32, (t, t), 1)
    prefix_ones = (row <= col).astype(BF16)
    carry = jnp.zeros((N_HEADS, 1), F32)
    for n in range(seq // t):
        blk = log_f[:, n * t:(n + 1) * t]
        hi = blk.astype(BF16)
        lo = (blk - hi.astype(F32)).astype(BF16)
        cs = (jnp.dot(hi, prefix_ones, preferred_element_type=F32)
              + jnp.dot(lo, prefix_ones, preferred_element_type=F32) + carry)
        o_ref[0, :, n * t:(n + 1) * t] = cs * LOG2_E
        carry = cs[:, t - 1:t]


def _forget_gates(h3, gain, w_f_t, b_f):
    batch, seq, d = h3.shape
    return pl.pallas_call(
        _forget_gate_kernel,
        grid=(batch,),
        in_specs=[
            pl.BlockSpec((1, seq, d), lambda b: (b, 0, 0)),
            pl.BlockSpec((1, d), lambda b: (0, 0)),
            pl.BlockSpec((N_HEADS, d), lambda b: (0, 0)),
            pl.BlockSpec((N_HEADS, 1), lambda b: (0, 0)),
        ],
        out_specs=pl.BlockSpec((1, N_HEADS, seq), lambda b: (b, 0, 0)),
        out_shape=jax.ShapeDtypeStruct((batch, N_HEADS, seq), F32),
        compiler_params=_params("parallel"),
        name="forget_gates",
    )(h3, gain.reshape(1, d), w_f_t, b_f.reshape(N_HEADS, 1))


def _fox_attn_kernel(q_ref, k_ref, v_ref, c_ref, o_ref, ps_ref, ms_ref, ls_ref, acc_ref):
    t = ATTN_TILE
    heads, groups = ps_ref.shape[0], acc_ref.shape[0]
    i = pl.program_id(2)
    head_ids = _head_ids(t)
    q_heads = _masked_queries(q_ref, head_ids, groups)
    row = lax.broadcasted_iota(jnp.int32, (t, t), 0)
    col = lax.broadcasted_iota(jnp.int32, (t, t), 1)
    causal = col <= row

    def probs(j, diagonal):
        cb = c_ref[0, 0, j]
        ms = [ms_ref[h] for h in range(heads)]
        ls = [ls_ref[h] for h in range(heads)]
        ps, alphas = [], []
        for h, qk in enumerate(_key_scores(q_heads, k_ref, _block_rows(j))):
            s = qk - cb[h:h + 1, :]
            if diagonal:
                s = jnp.where(causal, s, -jnp.inf)
            m_new = jnp.maximum(ms[h], jnp.max(s, axis=1, keepdims=True))
            p = jnp.exp2(s - _lane_tiled(m_new, t))
            alpha = jnp.exp2(ms[h] - m_new)
            ps.append(p.astype(BF16))
            ms[h] = m_new
            ls[h] = alpha * ls[h] + jnp.sum(p, axis=1, keepdims=True)
            alphas.append(alpha)
        for h in range(heads):
            ps_ref[h] = ps[h]
            ms_ref[h] = ms[h]
            ls_ref[h] = ls[h]
        return alphas

    def group_lanes_of(cols):
        return [_per_head_lanes(head_ids, cols[g * HEADS_PER_GROUP:(g + 1) * HEADS_PER_GROUP])
                for g in range(groups)]

    def trip(n, carry):
        j = i - 1 - n
        pvs = _weighted_values(ps_ref, v_ref, _block_rows(j + 1), head_ids)
        alphas = probs(j, False)
        for g, a in enumerate(group_lanes_of(alphas)):
            acc_ref[g] = (acc_ref[g] + pvs[g]) * a
        return carry

    ms_ref[...] = jnp.full(ms_ref.shape, -jnp.inf, F32)
    ls_ref[...] = jnp.zeros(ls_ref.shape, F32)
    acc_ref[...] = jnp.zeros(acc_ref.shape, F32)
    probs(i, True)
    lax.fori_loop(0, i, trip, 0)
    pvs = _weighted_values(ps_ref, v_ref, _block_rows(0), head_ids)
    denoms = group_lanes_of([ls_ref[h] for h in range(heads)])
    for g in range(groups):
        o_ref[0, :, _group_lanes(g)] = ((acc_ref[g] + pvs[g]) / denoms[g]).astype(o_ref.dtype)


def _fox_attention(q, kv, c_blocks, batch, seq):
    t = ATTN_TILE
    step_lanes = FOX_GROUPS * MXU_LANES
    v_off = D_MODEL // step_lanes
    return pl.pallas_call(
        _fox_attn_kernel,
        grid=(batch, D_MODEL // step_lanes, seq // t),
        in_specs=[
            pl.BlockSpec((1, t, step_lanes), lambda b, p, i: (b, i, p)),
            pl.BlockSpec((1, seq, step_lanes), lambda b, p, i: (b, 0, p)),
            pl.BlockSpec((1, seq, step_lanes), lambda b, p, i: (b, 0, v_off + p)),
            pl.BlockSpec((1, 1, seq // t, FOX_GROUPS * HEADS_PER_GROUP, t), lambda b, p, i: (b, p, 0, 0, 0)),
        ],
        out_specs=pl.BlockSpec((1, t, step_lanes), lambda b, p, i: (b, i, p)),
        out_shape=jax.ShapeDtypeStruct((batch, seq, D_MODEL), BF16),
        scratch_shapes=_attn_scratch(n_stats=2, groups=FOX_GROUPS),
        compiler_params=_params("parallel", "parallel", "arbitrary"),
        name="fox_attention",
    )(q, kv, kv, c_blocks)


def _first_index_of_max(x, lane, valid):
    masked = jnp.where(valid, x, -jnp.inf)
    top = jnp.max(masked, axis=1, keepdims=True)
    idx = jnp.min(jnp.where(masked == top, lane, LANES), axis=1, keepdims=True)
    return top, idx


def _router_kernel(h_ref, g_ref, whi_ref, wlo_ref, xn_ref, cmb_ref):
    xn = _rms_normed(h_ref[...], g_ref[...])
    x_hi = xn.astype(BF16)
    xn_ref[...] = x_hi
    x_lo = (xn - x_hi.astype(F32)).astype(BF16)
    logits = (jnp.dot(x_hi, whi_ref[...], preferred_element_type=F32)
              + jnp.dot(x_lo, whi_ref[...], preferred_element_type=F32)
              + jnp.dot(x_hi, wlo_ref[...], preferred_element_type=F32))
    lane = lax.broadcasted_iota(jnp.int32, logits.shape, 1)
    is_group = (lane >= N_EXPERTS) & (lane < N_EXPERTS + N_GROUPS)
    g_top, g_lane = _first_index_of_max(logits, lane, is_group)
    g_sum = jnp.sum(jnp.where(is_group, jnp.exp(logits - g_top), 0.0), axis=1, keepdims=True)
    g_prob = 1.0 / g_sum
    g_idx = g_lane - N_EXPERTS
    in_group = (lane >= g_idx * EXPERTS_PER_GROUP) & (lane < (g_idx + 1) * EXPERTS_PER_GROUP)
    t1, i1 = _first_index_of_max(logits, lane, in_group)
    t2, i2 = _first_index_of_max(logits, lane, in_group & (lane != i1))
    e2 = jnp.exp(t2 - t1)
    w1 = g_prob / (1.0 + e2)
    w2 = g_prob * e2 / (1.0 + e2)
    cmb_ref[...] = jnp.where(lane == i1, w1, 0.0) + jnp.where(lane == i2, w2, 0.0)


def _router(h, gain, w_hi, w_lo):
    n, d = h.shape
    return pl.pallas_call(
        _router_kernel,
        grid=(n // ROW_TILE,),
        in_specs=[
            pl.BlockSpec((ROW_TILE, d), lambda i: (i, 0)),
            pl.BlockSpec((1, d), lambda i: (0, 0)),
            pl.BlockSpec((d, LANES), lambda i: (0, 0)),
            pl.BlockSpec((d, LANES), lambda i: (0, 0)),
        ],
        out_specs=[
            pl.BlockSpec((ROW_TILE, d), lambda i: (i, 0)),
            pl.BlockSpec((ROW_TILE, LANES), lambda i: (i, 0)),
        ],
        out_shape=[
            jax.ShapeDtypeStruct((n, d), BF16),
            jax.ShapeDtypeStruct((n, LANES), F32),
        ],
        compiler_params=_params("parallel"),
        name="moe_router",
    )(h, gain.reshape(1, d), w_hi, w_lo)


def _moe_dense_kernel(xn_ref, cmb_ref, h_ref, wg_ref, wu_ref, wd_ref, fg_ref, o_ref, *, final_norm):
    e = pl.program_id(1)

    @pl.when(e == 0)
    def _():
        o_ref[...] = h_ref[...]

    x = xn_ref[...]
    gate = jnp.dot(x, wg_ref[0, 0].astype(BF16), preferred_element_type=F32)
    up = jnp.dot(x, wu_ref[0, 0].astype(BF16), preferred_element_type=F32)
    cmb = cmb_ref[...]
    lane = lax.broadcasted_iota(jnp.int32, cmb.shape, 1)
    scale = jnp.sum(jnp.where(lane == e, cmb, 0.0), axis=1, keepdims=True)
    act = gate / (1.0 + jnp.exp(-gate)) * up * scale
    o_ref[...] += jnp.dot(act.astype(BF16), wd_ref[0, 0].astype(BF16), preferred_element_type=F32)

    if final_norm:
        @pl.when(e == pl.num_programs(1) - 1)
        def _():
            o_ref[...] = _rms_normed(o_ref[...], fg_ref[...])


def _moe_dense(xn, cmb, h, layer, w_gate, w_up, w_down, final_gain):
    n, d = h.shape
    f = w_gate.shape[3]
    rows = MOE_ROW_TILE
    final_norm = final_gain is not None
    gain = final_gain if final_norm else jnp.ones((d,), F32)
    return pl.pallas_call(
        functools.partial(_moe_dense_kernel, final_norm=final_norm),
        grid=(n // rows, N_EXPERTS),
        in_specs=[
            pl.BlockSpec((rows, d), lambda i, e: (i, 0)),
            pl.BlockSpec((rows, LANES), lambda i, e: (i, 0)),
            pl.BlockSpec((rows, d), lambda i, e: (i, 0)),
            pl.BlockSpec((1, 1, d, f), lambda i, e: (layer, e, 0, 0)),
            pl.BlockSpec((1, 1, d, f), lambda i, e: (layer, e, 0, 0)),
            pl.BlockSpec((1, 1, f, d), lambda i, e: (layer, e, 0, 0)),
            pl.BlockSpec((1, d), lambda i, e: (0, 0)),
        ],
        out_specs=pl.BlockSpec((rows, d), lambda i, e: (i, 0)),
        out_shape=jax.ShapeDtypeStruct((n, d), F32),
        compiler_params=pltpu.CompilerParams(dimension_semantics=("parallel", "arbitrary"),
                                             vmem_limit_bytes=MOE_VMEM_LIMIT),
        name="moe_experts",
    )(xn, cmb, h, w_gate, w_up, w_down, gain.reshape(1, d))


def _moe_layer(h, gain, w_group, w_router, layer, w_gate, w_up, w_down, final_gain):
    d = h.shape[1]
    w_r = jnp.zeros((d, LANES), F32)
    w_r = w_r.at[:, :N_EXPERTS].set(w_router).at[:, N_EXPERTS:N_EXPERTS + N_GROUPS].set(w_group)
    w_hi = w_r.astype(BF16)
    w_lo = (w_r - w_hi.astype(F32)).astype(BF16)
    xn, cmb = _router(h, gain, w_hi, w_lo)
    return _moe_dense(xn, cmb, h, layer, w_gate, w_up, w_down, final_gain)


def kernel(x, attn_norm, w_qkv_a, w_o_a, kv_norm, w_kvf, b_f, w_q_b, w_o_b, moe_norm,
           w_group, w_router, w_gate, w_up, w_down, final_norm):
    batch, seq, d = x.shape
    n = batch * seq
    q_scale = HEAD_DIM ** -0.5 * LOG2_E
    h = x.reshape(n, d)
    n_a = w_qkv_a.shape[0]
    depth = attn_norm.shape[0]
    kv = c_blocks = None
    for layer in range(depth):
        if layer < n_a:
            w = w_qkv_a[layer]
            w = jnp.concatenate([w[:, :d] * q_scale, w[:, d:]], axis=1).astype(BF16)
            qkv = _norm_matmul(h, attn_norm[layer], w)
            mix = _sb_attention(qkv.reshape(batch, seq, 3 * d), batch, seq)
            h = _matmul_residual(mix.reshape(n, d), w_o_a[layer].astype(BF16), h)
        else:
            j = layer - n_a
            if j == 0:
                kv = _norm_matmul(h, kv_norm, w_kvf[:, :2 * d].astype(BF16))
                w_f_t = w_kvf[:, 2 * d:].T.astype(BF16)
                cum = _forget_gates(h.reshape(batch, seq, d), kv_norm, w_f_t, b_f)
                t = ATTN_TILE
                heads = FOX_GROUPS * HEADS_PER_GROUP
                c_blocks = cum.reshape(batch, N_HEADS // heads, heads, seq // t, t).transpose(0, 1, 3, 2, 4)
                kv = kv.reshape(batch, seq, 2 * d)
            q = _norm_matmul(h, attn_norm[layer], (w_q_b[j] * q_scale).astype(BF16))
            mix = _fox_attention(q.reshape(batch, seq, d), kv, c_blocks, batch, seq)
            h = _matmul_residual(mix.reshape(n, d), w_o_b[j].astype(BF16), h)
        final_gain = final_norm if layer == depth - 1 else None
        h = _moe_layer(h, moe_norm[layer], w_group[layer], w_router[layer], layer,
                       w_gate, w_up, w_down, final_gain)
    return h.reshape(batch, seq, d)
```

```python
import functools

import jax
import jax.numpy as jnp
from jax import lax
from jax.experimental import pallas as pl
from jax.experimental.pallas import tpu as pltpu

D_MODEL = 1024
N_HEADS = 16
HEAD_DIM = 64
N_GROUPS = 4
EXPERTS_PER_GROUP = 4
N_EXPERTS = 16
D_EXPERT = 512
RMS_EPS = 1e-6

LANES = 128
MXU_LANES = 256
HEADS_PER_GROUP = MXU_LANES // HEAD_DIM
SB_GROUPS = 1
FOX_GROUPS = 4
LOG2_E = 1.4426950408889634
SOFTPLUS_CLAMP = 64.0
ROW_TILE = 512
MOE_ROW_TILE = 1024
ATTN_TILE = 256
VMEM_LIMIT = 48 * 1024 * 1024
MOE_VMEM_LIMIT = 56 * 1024 * 1024

F32 = jnp.float32
BF16 = jnp.bfloat16
NT_DIMS = (((1,), (1,)), ((), ()))


def _params(*semantics):
    return pltpu.CompilerParams(dimension_semantics=semantics, vmem_limit_bytes=VMEM_LIMIT)


def _rms_normed(x, gain):
    var = jnp.mean(x * x, axis=-1, keepdims=True)
    return x * lax.rsqrt(var + RMS_EPS) * gain


def _norm_matmul_kernel(h_ref, g_ref, w_ref, o_ref):
    xn = _rms_normed(h_ref[...], g_ref[...]).astype(BF16)
    o_ref[...] = jnp.dot(xn, w_ref[...], preferred_element_type=F32).astype(o_ref.dtype)


def _norm_matmul(h, gain, w):
    n, d = h.shape
    n_out = w.shape[1]
    return pl.pallas_call(
        _norm_matmul_kernel,
        grid=(n // ROW_TILE,),
        in_specs=[
            pl.BlockSpec((ROW_TILE, d), lambda i: (i, 0)),
            pl.BlockSpec((1, d), lambda i: (0, 0)),
            pl.BlockSpec((d, n_out), lambda i: (0, 0)),
        ],
        out_specs=pl.BlockSpec((ROW_TILE, n_out), lambda i: (i, 0)),
        out_shape=jax.ShapeDtypeStruct((n, n_out), BF16),
        compiler_params=_params("parallel"),
        name="norm_matmul",
    )(h, gain.reshape(1, d), w)


def _matmul_residual_kernel(mix_ref, w_ref, h_ref, o_ref):
    o_ref[...] = h_ref[...] + jnp.dot(mix_ref[...], w_ref[...], preferred_element_type=F32)


def _matmul_residual(mix, w, h):
    n, d = h.shape
    return pl.pallas_call(
        _matmul_residual_kernel,
        grid=(n // ROW_TILE,),
        in_specs=[
            pl.BlockSpec((ROW_TILE, d), lambda i: (i, 0)),
            pl.BlockSpec((d, d), lambda i: (0, 0)),
            pl.BlockSpec((ROW_TILE, d), lambda i: (i, 0)),
        ],
        out_specs=pl.BlockSpec((ROW_TILE, d), lambda i: (i, 0)),
        out_shape=jax.ShapeDtypeStruct((n, d), F32),
        compiler_params=_params("parallel"),
        name="matmul_residual",
    )(mix, w, h)


def _head_ids(rows):
    lane = lax.broadcasted_iota(jnp.int32, (rows, MXU_LANES), 1)
    return lane // HEAD_DIM


def _only_head(x, head_ids, h):
    return jnp.where(head_ids == h, x, jnp.zeros_like(x))


def _group_lanes(g):
    return slice(g * MXU_LANES, (g + 1) * MXU_LANES)


def _masked_queries(q_ref, head_ids, groups):
    return [_only_head(q_ref[0, :, _group_lanes(g)], head_ids, h)
            for g in range(groups) for h in range(HEADS_PER_GROUP)]


def _key_scores(q_heads, k_ref, rows):
    out = []
    for g in range(len(q_heads) // HEADS_PER_GROUP):
        kb = k_ref[0, rows, _group_lanes(g)]
        for h in range(HEADS_PER_GROUP):
            out.append(lax.dot_general(q_heads[g * HEADS_PER_GROUP + h], kb, NT_DIMS,
                                       preferred_element_type=F32))
    return out


def _weighted_values(ws_ref, v_ref, rows, head_ids):
    out = []
    for g in range(ws_ref.shape[0] // HEADS_PER_GROUP):
        vb = v_ref[0, rows, _group_lanes(g)]
        acc = None
        for h in range(HEADS_PER_GROUP):
            pv = jnp.dot(ws_ref[g * HEADS_PER_GROUP + h], _only_head(vb, head_ids, h),
                         preferred_element_type=F32)
            acc = pv if acc is None else acc + pv
        out.append(acc)
    return out


def _attn_scratch(n_stats, groups):
    t = ATTN_TILE
    heads = groups * HEADS_PER_GROUP
    return ([pltpu.VMEM((heads, t, t), BF16)]
            + [pltpu.VMEM((heads, t, LANES), F32)] * n_stats
            + [pltpu.VMEM((groups, t, MXU_LANES), F32)])


def _lane_tiled(x, width):
    return jnp.tile(x, (1, width // LANES))


def _per_head_lanes(head_ids, cols):
    cols = [_lane_tiled(c, MXU_LANES) for c in cols]
    out = cols[-1]
    for h in range(HEADS_PER_GROUP - 2, -1, -1):
        out = jnp.where(head_ids == h, cols[h], out)
    return out


def _block_rows(j):
    return pl.ds(pl.multiple_of(j * ATTN_TILE, ATTN_TILE), ATTN_TILE)


def _sb_attn_kernel(q_ref, k_ref, v_ref, o_ref, ws_ref, cs_ref, acc_ref):
    t = ATTN_TILE
    heads, groups = ws_ref.shape[0], acc_ref.shape[0]
    i = pl.program_id(2)
    head_ids = _head_ids(t)
    q_heads = _masked_queries(q_ref, head_ids, groups)
    row = lax.broadcasted_iota(jnp.int32, (t, t), 0)
    col = lax.broadcasted_iota(jnp.int32, (t, t), 1)
    strict = col < row
    neg_suffix_ones = jnp.where(row >= col, -1.0, 0.0).astype(BF16)

    def weights(j, diagonal):
        cs = [cs_ref[h] for h in range(heads)]
        ws = []
        for h, z in enumerate(_key_scores(q_heads, k_ref, _block_rows(j))):
            sp = jnp.maximum(z, jnp.log(1.0 + jnp.exp2(jnp.minimum(z, SOFTPLUS_CLAMP))) * LOG2_E)
            if diagonal:
                sp = jnp.where(strict, sp, 0.0)
            suffix = jnp.dot(sp.astype(BF16), neg_suffix_ones, preferred_element_type=F32)
            w = jnp.exp2(z + suffix + _lane_tiled(cs[h], t))
            if diagonal:
                w = jnp.where(strict, w, 0.0)
            ws.append(w.astype(BF16))
            cs[h] = cs[h] - jnp.sum(sp, axis=1, keepdims=True)
        for h in range(heads):
            ws_ref[h] = ws[h]
            cs_ref[h] = cs[h]

    def add_pv(j):
        for g, pv in enumerate(_weighted_values(ws_ref, v_ref, _block_rows(j), head_ids)):
            acc_ref[g] += pv

    def trip(n, carry):
        j = i - 1 - n
        add_pv(j + 1)
        weights(j, False)
        return carry

    cs_ref[...] = jnp.zeros(cs_ref.shape, F32)
    acc_ref[...] = jnp.zeros(acc_ref.shape, F32)
    weights(i, True)
    lax.fori_loop(0, i, trip, 0)
    add_pv(0)
    for g in range(groups):
        o_ref[0, :, _group_lanes(g)] = acc_ref[g].astype(o_ref.dtype)


def _sb_attention(qkv, batch, seq):
    t = ATTN_TILE
    step_lanes = SB_GROUPS * MXU_LANES
    k_off = D_MODEL // step_lanes
    return pl.pallas_call(
        _sb_attn_kernel,
        grid=(batch, D_MODEL // step_lanes, seq // t),
        in_specs=[
            pl.BlockSpec((1, t, step_lanes), lambda b, p, i: (b, i, p)),
            pl.BlockSpec((1, seq, step_lanes), lambda b, p, i: (b, 0, k_off + p)),
            pl.BlockSpec((1, seq, step_lanes), lambda b, p, i: (b, 0, 2 * k_off + p)),
        ],
        out_specs=pl.BlockSpec((1, t, step_lanes), lambda b, p, i: (b, i, p)),
        out_shape=jax.ShapeDtypeStruct((batch, seq, D_MODEL), BF16),
        scratch_shapes=_attn_scratch(n_stats=1, groups=SB_GROUPS),
        compiler_params=_params("parallel", "parallel", "arbitrary"),
        name="sb_attention",
    )(qkv, qkv, qkv)


def _forget_gate_kernel(h_ref, g_ref, wf_ref, bf_ref, o_ref):
    t = ATTN_TILE
    seq = h_ref.shape[1]
    u = _rms_normed(h_ref[0], g_ref[...]).astype(BF16)
    logits = lax.dot_general(wf_ref[...], u, NT_DIMS, preferred_element_type=F32) + bf_ref[...]
    log_f = jnp.minimum(logits, 0.0) - jnp.log(1.0 + jnp.exp(-jnp.abs(logits)))
    row = lax.broadcasted_iota(jnp.int32, (t, t), 0)
    col = lax.broadcasted_iota(jnp.int32, (t, t), 1)
    prefix_ones = (row <= col).astype(BF16)
    carry = jnp.zeros((N_HEADS, 1), F32)
    for n in range(seq // t):
        blk = log_f[:, n * t:(n + 1) * t]
        hi = blk.astype(BF16)
        lo = (blk - hi.astype(F32)).astype(BF16)
        cs = (jnp.dot(hi, prefix_ones, preferred_element_type=F32)
              + jnp.dot(lo, prefix_ones, preferred_element_type=F32) + carry)
        o_ref[0, :, n * t:(n + 1) * t] = cs * LOG2_E
        carry = cs[:, t - 1:t]


def _forget_gates(h3, gain, w_f_t, b_f):
    batch, seq, d = h3.shape
    return pl.pallas_call(
        _forget_gate_kernel,
        grid=(batch,),
        in_specs=[
            pl.BlockSpec((1, seq, d), lambda b: (b, 0, 0)),
            pl.BlockSpec((1, d), lambda b: (0, 0)),
            pl.BlockSpec((N_HEADS, d), lambda b: (0, 0)),
            pl.BlockSpec((N_HEADS, 1), lambda b: (0, 0)),
        ],
        out_specs=pl.BlockSpec((1, N_HEADS, seq), lambda b: (b, 0, 0)),
        out_shape=jax.ShapeDtypeStruct((batch, N_HEADS, seq), F32),
        compiler_params=_params("parallel"),
        name="forget_gates",
    )(h3, gain.reshape(1, d), w_f_t, b_f.reshape(N_HEADS, 1))


def _fox_attn_kernel(q_ref, k_ref, v_ref, c_ref, o_ref, ps_ref, ms_ref, ls_ref, acc_ref):
    t = ATTN_TILE
    heads, groups = ps_ref.shape[0], acc_ref.shape[0]
    i = pl.program_id(2)
    head_ids = _head_ids(t)
    q_heads = _masked_queries(q_ref, head_ids, groups)
    row = lax.broadcasted_iota(jnp.int32, (t, t), 0)
    col = lax.broadcasted_iota(jnp.int32, (t, t), 1)
    causal = col <= row

    def probs(j, diagonal):
        cb = c_ref[0, 0, j]
        ms = [ms_ref[h] for h in range(heads)]
        ls = [ls_ref[h] for h in range(heads)]
        ps, alphas = [], []
        for h, qk in enumerate(_key_scores(q_heads, k_ref, _block_rows(j))):
            s = qk - cb[h:h + 1, :]
            if diagonal:
                s = jnp.where(causal, s, -jnp.inf)
            m_new = jnp.maximum(ms[h], jnp.max(s, axis=1, keepdims=True))
            p = jnp.exp2(s - _lane_tiled(m_new, t))
            alpha = jnp.exp2(ms[h] - m_new)
            ps.append(p.astype(BF16))
            ms[h] = m_new
            ls[h] = alpha * ls[h] + jnp.sum(p, axis=1, keepdims=True)
            alphas.append(alpha)
        for h in range(heads):
            ps_ref[h] = ps[h]
            ms_ref[h] = ms[h]
            ls_ref[h] = ls[h]
        return alphas

    def group_lanes_of(cols):
        return [_per_head_lanes(head_ids, cols[g * HEADS_PER_GROUP:(g + 1) * HEADS_PER_GROUP])
                for g in range(groups)]

    def trip(n, carry):
        j = i - 1 - n
        pvs = _weighted_values(ps_ref, v_ref, _block_rows(j + 1), head_ids)
        alphas = probs(j, False)
        for g, a in enumerate(group_lanes_of(alphas)):
            acc_ref[g] = (acc_ref[g] + pvs[g]) * a
        return carry

    ms_ref[...] = jnp.full(ms_ref.shape, -jnp.inf, F32)
    ls_ref[...] = jnp.zeros(ls_ref.shape, F32)
    acc_ref[...] = jnp.zeros(acc_ref.shape, F32)
    probs(i, True)
    lax.fori_loop(0, i, trip, 0)
    pvs = _weighted_values(ps_ref, v_ref, _block_rows(0), head_ids)
    denoms = group_lanes_of([ls_ref[h] for h in range(heads)])
    for g in range(groups):
        o_ref[0, :, _group_lanes(g)] = ((acc_ref[g] + pvs[g]) / denoms[g]).astype(o_ref.dtype)


def _fox_attention(q, kv, c_blocks, batch, seq):
    t = ATTN_TILE
    step_lanes = FOX_GROUPS * MXU_LANES
    v_off = D_MODEL // step_lanes
    return pl.pallas_call(
        _fox_attn_kernel,
        grid=(batch, D_MODEL // step_lanes, seq // t),
        in_specs=[
            pl.BlockSpec((1, t, step_lanes), lambda b, p, i: (b, i, p)),
            pl.BlockSpec((1, seq, step_lanes), lambda b, p, i: (b, 0, p)),
            pl.BlockSpec((1, seq, step_lanes), lambda b, p, i: (b, 0, v_off + p)),
            pl.BlockSpec((1, 1, seq // t, FOX_GROUPS * HEADS_PER_GROUP, t), lambda b, p, i: (b, p, 0, 0, 0)),
        ],
        out_specs=pl.BlockSpec((1, t, step_lanes), lambda b, p, i: (b, i, p)),
        out_shape=jax.ShapeDtypeStruct((batch, seq, D_MODEL), BF16),
        scratch_shapes=_attn_scratch(n_stats=2, groups=FOX_GROUPS),
        compiler_params=_params("parallel", "parallel", "arbitrary"),
        name="fox_attention",
    )(q, kv, kv, c_blocks)


def _first_index_of_max(x, lane, valid):
    masked = jnp.where(valid, x, -jnp.inf)
    top = jnp.max(masked, axis=1, keepdims=True)
    idx = jnp.min(jnp.where(masked == top, lane, LANES), axis=1, keepdims=True)
    return top, idx


def _router_kernel(h_ref, g_ref, whi_ref, wlo_ref, xn_ref, cmb_ref):
    xn = _rms_normed(h_ref[...], g_ref[...])
    x_hi = xn.astype(BF16)
    xn_ref[...] = x_hi
    x_lo = (xn - x_hi.astype(F32)).astype(BF16)
    logits = (jnp.dot(x_hi, whi_ref[...], preferred_element_type=F32)
              + jnp.dot(x_lo, whi_ref[...], preferred_element_type=F32)
              + jnp.dot(x_hi, wlo_ref[...], preferred_element_type=F32))
    lane = lax.broadcasted_iota(jnp.int32, logits.shape, 1)
    is_group = (lane >= N_EXPERTS) & (lane < N_EXPERTS + N_GROUPS)
    g_top, g_lane = _first_index_of_max(logits, lane, is_group)
    g_sum = jnp.sum(jnp.where(is_group, jnp.exp(logits - g_top), 0.0), axis=1, keepdims=True)
    g_prob = 1.0 / g_sum
    g_idx = g_lane - N_EXPERTS
    in_group = (lane >= g_idx * EXPERTS_PER_GROUP) & (lane < (g_idx + 1) * EXPERTS_PER_GROUP)
    t1, i1 = _first_index_of_max(logits, lane, in_group)
    t2, i2 = _first_index_of_max(logits, lane, in_group & (lane != i1))
    e2 = jnp.exp(t2 - t1)
    w1 = g_prob / (1.0 + e2)
    w2 = g_prob * e2 / (1.0 + e2)
    cmb_ref[...] = jnp.where(lane == i1, w1, 0.0) + jnp.where(lane == i2, w2, 0.0)


def _router(h, gain, w_hi, w_lo):
    n, d = h.shape
    return pl.pallas_call(
        _router_kernel,
        grid=(n // ROW_TILE,),
        in_specs=[
            pl.BlockSpec((ROW_TILE, d), lambda i: (i, 0)),
            pl.BlockSpec((1, d), lambda i: (0, 0)),
            pl.BlockSpec((d, LANES), lambda i: (0, 0)),
            pl.BlockSpec((d, LANES), lambda i: (0, 0)),
        ],
        out_specs=[
            pl.BlockSpec((ROW_TILE, d), lambda i: (i, 0)),
            pl.BlockSpec((ROW_TILE, LANES), lambda i: (i, 0)),
        ],
        out_shape=[
            jax.ShapeDtypeStruct((n, d), BF16),
            jax.ShapeDtypeStruct((n, LANES), F32),
        ],
        compiler_params=_params("parallel"),
        name="moe_router",
    )(h, gain.reshape(1, d), w_hi, w_lo)


def _moe_dense_kernel(xn_ref, cmb_ref, h_ref, wg_ref, wu_ref, wd_ref, fg_ref, o_ref, *, final_norm):
    e = pl.program_id(1)

    @pl.when(e == 0)
    def _():
        o_ref[...] = h_ref[...]

    x = xn_ref[...]
    gate = jnp.dot(x, wg_ref[0, 0].astype(BF16), preferred_element_type=F32)
    up = jnp.dot(x, wu_ref[0, 0].astype(BF16), preferred_element_type=F32)
    cmb = cmb_ref[...]
    lane = lax.broadcasted_iota(jnp.int32, cmb.shape, 1)
    scale = jnp.sum(jnp.where(lane == e, cmb, 0.0), axis=1, keepdims=True)
    act = gate / (1.0 + jnp.exp(-gate)) * up * scale
    o_ref[...] += jnp.dot(act.astype(BF16), wd_ref[0, 0].astype(BF16), preferred_element_type=F32)

    if final_norm:
        @pl.when(e == pl.num_programs(1) - 1)
        def _():
            o_ref[...] = _rms_normed(o_ref[...], fg_ref[...])


def _moe_dense(xn, cmb, h, layer, w_gate, w_up, w_down, final_gain):
    n, d = h.shape
    f = w_gate.shape[3]
    rows = MOE_ROW_TILE
    final_norm = final_gain is not None
    gain = final_gain if final_norm else jnp.ones((d,), F32)
    return pl.pallas_call(
        functools.partial(_moe_dense_kernel, final_norm=final_norm),
        grid=(n // rows, N_EXPERTS),
        in_specs=[
            pl.BlockSpec((rows, d), lambda i, e: (i, 0)),
            pl.BlockSpec((rows, LANES), lambda i, e: (i, 0)),
            pl.BlockSpec((rows, d), lambda i, e: (i, 0)),
            pl.BlockSpec((1, 1, d, f), lambda i, e: (layer, e, 0, 0)),
            pl.BlockSpec((1, 1, d, f), lambda i, e: (layer, e, 0, 0)),
            pl.BlockSpec((1, 1, f, d), lambda i, e: (layer, e, 0, 0)),
            pl.BlockSpec((1, d), lambda i, e: (0, 0)),
        ],
        out_specs=pl.BlockSpec((rows, d), lambda i, e: (i, 0)),
        out_shape=jax.ShapeDtypeStruct((n, d), F32),
        compiler_params=pltpu.CompilerParams(dimension_semantics=("parallel", "arbitrary"),
                                             vmem_limit_bytes=MOE_VMEM_LIMIT),
        name="moe_experts",
    )(xn, cmb, h, w_gate, w_up, w_down, gain.reshape(1, d))


def _moe_layer(h, gain, w_group, w_router, layer, w_gate, w_up, w_down, final_gain):
    d = h.shape[1]
    w_r = jnp.zeros((d, LANES), F32)
    w_r = w_r.at[:, :N_EXPERTS].set(w_router).at[:, N_EXPERTS:N_EXPERTS + N_GROUPS].set(w_group)
    w_hi = w_r.astype(BF16)
    w_lo = (w_r - w_hi.astype(F32)).astype(BF16)
    xn, cmb = _router(h, gain, w_hi, w_lo)
    return _moe_dense(xn, cmb, h, layer, w_gate, w_up, w_down, final_gain)


def kernel(x, attn_norm, w_qkv_a, w_o_a, kv_norm, w_kvf, b_f, w_q_b, w_o_b, moe_norm,
           w_group, w_router, w_gate, w_up, w_down, final_norm):
    batch, seq, d = x.shape
    n = batch * seq
    q_scale = HEAD_DIM ** -0.5 * LOG2_E
    h = x.reshape(n, d)
    n_a = w_qkv_a.shape[0]
    depth = attn_norm.shape[0]
    kv = c_blocks = None
    for layer in range(depth):
        if layer < n_a:
            w = w_qkv_a[layer]
            w = jnp.concatenate([w[:, :d] * q_scale, w[:, d:]], axis=1).astype(BF16)
            qkv = _norm_matmul(h, attn_norm[layer], w)
            mix = _sb_attention(qkv.reshape(batch, seq, 3 * d), batch, seq)
            h = _matmul_residual(mix.reshape(n, d), w_o_a[layer].astype(BF16), h)
        else:
            j = layer - n_a
            if j == 0:
                kv = _norm_matmul(h, kv_norm, w_kvf[:, :2 * d].astype(BF16))
                w_f_t = w_kvf[:, 2 * d:].T.astype(BF16)
                cum = _forget_gates(h.reshape(batch, seq, d), kv_norm, w_f_t, b_f)
                t = ATTN_TILE
                heads = FOX_GROUPS * HEADS_PER_GROUP
                c_blocks = cum.reshape(batch, N_HEADS // heads, heads, seq // t, t).transpose(0, 1, 3, 2, 4)
                kv = kv.reshape(batch, seq, 2 * d)
            q = _norm_matmul(h, attn_norm[layer], (w_q_b[j] * q_scale).astype(BF16))
            mix = _fox_attention(q.reshape(batch, seq, d), kv, c_blocks, batch, seq)
            h = _matmul_residual(mix.reshape(n, d), w_o_b[j].astype(BF16), h)
        final_gain = final_norm if layer == depth - 1 else None
        h = _moe_layer(h, moe_norm[layer], w_group[layer], w_router[layer], layer,
                       w_gate, w_up, w_down, final_gain)
    return h.reshape(batch, seq, d)
```

```python
import functools

import jax
import jax.numpy as jnp
from jax import lax
from jax.experimental import pallas as pl
from jax.experimental.pallas import tpu as pltpu

D_MODEL = 1024
N_HEADS = 16
HEAD_DIM = 64
N_GROUPS = 4
EXPERTS_PER_GROUP = 4
N_EXPERTS = 16
D_EXPERT = 512
RMS_EPS = 1e-6

LANES = 128
MXU_LANES = 256
HEADS_PER_GROUP = MXU_LANES // HEAD_DIM
SB_GROUPS = 2
FOX_GROUPS = 4
LOG2_E = 1.4426950408889634
SOFTPLUS_CLAMP = 64.0
ROW_TILE = 512
MOE_ROW_TILE = 1024
ATTN_TILE = 256
VMEM_LIMIT = 48 * 1024 * 1024
MOE_VMEM_LIMIT = 56 * 1024 * 1024

F32 = jnp.float32
BF16 = jnp.bfloat16
NT_DIMS = (((1,), (1,)), ((), ()))


def _params(*semantics):
    return pltpu.CompilerParams(dimension_semantics=semantics, vmem_limit_bytes=VMEM_LIMIT)


def _rms_normed(x, gain):
    var = jnp.mean(x * x, axis=-1, keepdims=True)
    return x * lax.rsqrt(var + RMS_EPS) * gain


def _norm_matmul_kernel(h_ref, g_ref, w_ref, o_ref):
    xn = _rms_normed(h_ref[...], g_ref[...]).astype(BF16)
    o_ref[...] = jnp.dot(xn, w_ref[...], preferred_element_type=F32).astype(o_ref.dtype)


def _norm_matmul(h, gain, w):
    n, d = h.shape
    n_out = w.shape[1]
    return pl.pallas_call(
        _norm_matmul_kernel,
        grid=(n // ROW_TILE,),
        in_specs=[
            pl.BlockSpec((ROW_TILE, d), lambda i: (i, 0)),
            pl.BlockSpec((1, d), lambda i: (0, 0)),
            pl.BlockSpec((d, n_out), lambda i: (0, 0)),
        ],
        out_specs=pl.BlockSpec((ROW_TILE, n_out), lambda i: (i, 0)),
        out_shape=jax.ShapeDtypeStruct((n, n_out), BF16),
        compiler_params=_params("parallel"),
        name="norm_matmul",
    )(h, gain.reshape(1, d), w)


def _matmul_residual_kernel(mix_ref, w_ref, h_ref, o_ref):
    o_ref[...] = h_ref[...] + jnp.dot(mix_ref[...], w_ref[...], preferred_element_type=F32)


def _matmul_residual(mix, w, h):
    n, d = h.shape
    return pl.pallas_call(
        _matmul_residual_kernel,
        grid=(n // ROW_TILE,),
        in_specs=[
            pl.BlockSpec((ROW_TILE, d), lambda i: (i, 0)),
            pl.BlockSpec((d, d), lambda i: (0, 0)),
            pl.BlockSpec((ROW_TILE, d), lambda i: (i, 0)),
        ],
        out_specs=pl.BlockSpec((ROW_TILE, d), lambda i: (i, 0)),
        out_shape=jax.ShapeDtypeStruct((n, d), F32),
        compiler_params=_params("parallel"),
        name="matmul_residual",
    )(mix, w, h)


def _head_ids(rows):
    lane = lax.broadcasted_iota(jnp.int32, (rows, MXU_LANES), 1)
    return lane // HEAD_DIM


def _only_head(x, head_ids, h):
    return jnp.where(head_ids == h, x, jnp.zeros_like(x))


def _group_lanes(g):
    return slice(g * MXU_LANES, (g + 1) * MXU_LANES)


def _masked_queries(q_ref, head_ids, groups):
    return [_only_head(q_ref[0, :, _group_lanes(g)], head_ids, h)
            for g in range(groups) for h in range(HEADS_PER_GROUP)]


def _key_scores(q_heads, k_ref, rows):
    out = []
    for g in range(len(q_heads) // HEADS_PER_GROUP):
        kb = k_ref[0, rows, _group_lanes(g)]
        for h in range(HEADS_PER_GROUP):
            out.append(lax.dot_general(q_heads[g * HEADS_PER_GROUP + h], kb, NT_DIMS,
                                       preferred_element_type=F32))
    return out


def _weighted_values(ws_ref, v_ref, rows, head_ids):
    out = []
    for g in range(ws_ref.shape[0] // HEADS_PER_GROUP):
        vb = v_ref[0, rows, _group_lanes(g)]
        acc = None
        for h in range(HEADS_PER_GROUP):
            pv = jnp.dot(ws_ref[g * HEADS_PER_GROUP + h], _only_head(vb, head_ids, h),
                         preferred_element_type=F32)
            acc = pv if acc is None else acc + pv
        out.append(acc)
    return out


def _attn_scratch(n_stats, groups):
    t = ATTN_TILE
    heads = groups * HEADS_PER_GROUP
    return ([pltpu.VMEM((heads, t, t), BF16)]
            + [pltpu.VMEM((heads, t, LANES), F32)] * n_stats
            + [pltpu.VMEM((groups, t, MXU_LANES), F32)])


def _lane_tiled(x, width):
    return jnp.tile(x, (1, width // LANES))


def _per_head_lanes(head_ids, cols):
    cols = [_lane_tiled(c, MXU_LANES) for c in cols]
    out = cols[-1]
    for h in range(HEADS_PER_GROUP - 2, -1, -1):
        out = jnp.where(head_ids == h, cols[h], out)
    return out


def _block_rows(j):
    return pl.ds(pl.multiple_of(j * ATTN_TILE, ATTN_TILE), ATTN_TILE)


def _sb_attn_kernel(q_ref, k_ref, v_ref, o_ref, ws_ref, cs_ref, acc_ref):
    t = ATTN_TILE
    heads, groups = ws_ref.shape[0], acc_ref.shape[0]
    i = pl.program_id(2)
    head_ids = _head_ids(t)
    q_heads = _masked_queries(q_ref, head_ids, groups)
    row = lax.broadcasted_iota(jnp.int32, (t, t), 0)
    col = lax.broadcasted_iota(jnp.int32, (t, t), 1)
    strict = col < row
    neg_suffix_ones = jnp.where(row >= col, -1.0, 0.0).astype(BF16)

    def weights(j, diagonal):
        cs = [cs_ref[h] for h in range(heads)]
        ws = []
        for h, z in enumerate(_key_scores(q_heads, k_ref, _block_rows(j))):
            sp = jnp.maximum(z, jnp.log(1.0 + jnp.exp2(jnp.minimum(z, SOFTPLUS_CLAMP))) * LOG2_E)
            if diagonal:
                sp = jnp.where(strict, sp, 0.0)
            suffix = jnp.dot(sp.astype(BF16), neg_suffix_ones, preferred_element_type=F32)
            w = jnp.exp2(z + suffix + _lane_tiled(cs[h], t))
            if diagonal:
                w = jnp.where(strict, w, 0.0)
            ws.append(w.astype(BF16))
            cs[h] = cs[h] - jnp.sum(sp, axis=1, keepdims=True)
        for h in range(heads):
            ws_ref[h] = ws[h]
            cs_ref[h] = cs[h]

    def add_pv(j):
        for g, pv in enumerate(_weighted_values(ws_ref, v_ref, _block_rows(j), head_ids)):
            acc_ref[g] += pv

    def trip(n, carry):
        j = i - 1 - n
        add_pv(j + 1)
        weights(j, False)
        return carry

    cs_ref[...] = jnp.zeros(cs_ref.shape, F32)
    acc_ref[...] = jnp.zeros(acc_ref.shape, F32)
    weights(i, True)
    lax.fori_loop(0, i, trip, 0)
    add_pv(0)
    for g in range(groups):
        o_ref[0, :, _group_lanes(g)] = acc_ref[g].astype(o_ref.dtype)


def _sb_attention(qkv, batch, seq):
    t = ATTN_TILE
    step_lanes = SB_GROUPS * MXU_LANES
    k_off = D_MODEL // step_lanes
    return pl.pallas_call(
        _sb_attn_kernel,
        grid=(batch, D_MODEL // step_lanes, seq // t),
        in_specs=[
            pl.BlockSpec((1, t, step_lanes), lambda b, p, i: (b, i, p)),
            pl.BlockSpec((1, seq, step_lanes), lambda b, p, i: (b, 0, k_off + p)),
            pl.BlockSpec((1, seq, step_lanes), lambda b, p, i: (b, 0, 2 * k_off + p)),
        ],
        out_specs=pl.BlockSpec((1, t, step_lanes), lambda b, p, i: (b, i, p)),
        out_shape=jax.ShapeDtypeStruct((batch, seq, D_MODEL), BF16),
        scratch_shapes=_attn_scratch(n_stats=1, groups=SB_GROUPS),
        compiler_params=_params("parallel", "parallel", "arbitrary"),
        name="sb_attention",
    )(qkv, qkv, qkv)


def _forget_gate_kernel(h_ref, g_ref, wf_ref, bf_ref, o_ref):
    t = ATTN_TILE
    seq = h_ref.shape[1]
    u = _rms_normed(h_ref[0], g_ref[...]).astype(BF16)
    logits = lax.dot_general(wf_ref[...], u, NT_DIMS, preferred_element_type=F32) + bf_ref[...]
    log_f = jnp.minimum(logits, 0.0) - jnp.log(1.0 + jnp.exp(-jnp.abs(logits)))
    row = lax.broadcasted_iota(jnp.int32, (t, t), 0)
    col = lax.broadcasted_iota(jnp.int32, (t, t), 1)
    prefix_ones = (row <= col).astype(BF16)
    carry = jnp.zeros((N_HEADS, 1), F32)
    for n in range(seq // t):
        blk = log_f[:, n * t:(n + 1) * t]
        hi = blk.astype(BF16)
        lo = (blk - hi.astype(F32)).astype(BF16)
        cs = (jnp.dot(hi, prefix_ones, preferred_element_type=F32)
              + jnp.dot(lo, prefix_ones, preferred_element_type=F32) + carry)
        o_ref[0, :, n * t:(n + 1) * t] = cs * LOG2_E
        carry = cs[:, t - 1:t]


def _forget_gates(h3, gain, w_f_t, b_f):
    batch, seq, d = h3.shape
    return pl.pallas_call(
        _forget_gate_kernel,
        grid=(batch,),
        in_specs=[
            pl.BlockSpec((1, seq, d), lambda b: (b, 0, 0)),
            pl.BlockSpec((1, d), lambda b: (0, 0)),
            pl.BlockSpec((N_HEADS, d), lambda b: (0, 0)),
            pl.BlockSpec((N_HEADS, 1), lambda b: (0, 0)),
        ],
        out_specs=pl.BlockSpec((1, N_HEADS, seq), lambda b: (b, 0, 0)),
        out_shape=jax.ShapeDtypeStruct((batch, N_HEADS, seq), F32),
        compiler_params=_params("parallel"),
        name="forget_gates",
    )(h3, gain.reshape(1, d), w_f_t, b_f.reshape(N_HEADS, 1))


def _fox_attn_kernel(q_ref, k_ref, v_ref, c_ref, o_ref, ps_ref, ms_ref, ls_ref, acc_ref):
    t = ATTN_TILE
    heads, groups = ps_ref.shape[0], acc_ref.shape[0]
    i = pl.program_id(2)
    head_ids = _head_ids(t)
    q_heads = _masked_queries(q_ref, head_ids, groups)
    row = lax.broadcasted_iota(jnp.int32, (t, t), 0)
    col = lax.broadcasted_iota(jnp.int32, (t, t), 1)
    causal = col <= row

    def probs(j, diagonal):
        cb = c_ref[0, 0, j]
        ms = [ms_ref[h] for h in range(heads)]
        ls = [ls_ref[h] for h in range(heads)]
        ps, alphas = [], []
        for h, qk in enumerate(_key_scores(q_heads, k_ref, _block_rows(j))):
            s = qk - cb[h:h + 1, :]
            if diagonal:
                s = jnp.where(causal, s, -jnp.inf)
            m_new = jnp.maximum(ms[h], jnp.max(s, axis=1, keepdims=True))
            p = jnp.exp2(s - _lane_tiled(m_new, t))
            alpha = jnp.exp2(ms[h] - m_new)
            ps.append(p.astype(BF16))
            ms[h] = m_new
            ls[h] = alpha * ls[h] + jnp.sum(p, axis=1, keepdims=True)
            alphas.append(alpha)
        for h in range(heads):
            ps_ref[h] = ps[h]
            ms_ref[h] = ms[h]
            ls_ref[h] = ls[h]
        return alphas

    def group_lanes_of(cols):
        return [_per_head_lanes(head_ids, cols[g * HEADS_PER_GROUP:(g + 1) * HEADS_PER_GROUP])
                for g in range(groups)]

    def trip(n, carry):
        j = i - 1 - n
        pvs = _weighted_values(ps_ref, v_ref, _block_rows(j + 1), head_ids)
        alphas = probs(j, False)
        for g, a in enumerate(group_lanes_of(alphas)):
            acc_ref[g] = (acc_ref[g] + pvs[g]) * a
        return carry

    ms_ref[...] = jnp.full(ms_ref.shape, -jnp.inf, F32)
    ls_ref[...] = jnp.zeros(ls_ref.shape, F32)
    acc_ref[...] = jnp.zeros(acc_ref.shape, F32)
    probs(i, True)
    lax.fori_loop(0, i, trip, 0)
    pvs = _weighted_values(ps_ref, v_ref, _block_rows(0), head_ids)
    denoms = group_lanes_of([ls_ref[h] for h in range(heads)])
    for g in range(groups):
        o_ref[0, :, _group_lanes(g)] = ((acc_ref[g] + pvs[g]) / denoms[g]).astype(o_ref.dtype)


def _fox_attention(q, kv, c_blocks, batch, seq):
    t = ATTN_TILE
    step_lanes = FOX_GROUPS * MXU_LANES
    v_off = D_MODEL // step_lanes
    return pl.pallas_call(
        _fox_attn_kernel,
        grid=(batch, D_MODEL // step_lanes, seq // t),
        in_specs=[
            pl.BlockSpec((1, t, step_lanes), lambda b, p, i: (b, i, p)),
            pl.BlockSpec((1, seq, step_lanes), lambda b, p, i: (b, 0, p)),
            pl.BlockSpec((1, seq, step_lanes), lambda b, p, i: (b, 0, v_off + p)),
            pl.BlockSpec((1, 1, seq // t, FOX_GROUPS * HEADS_PER_GROUP, t), lambda b, p, i: (b, p, 0, 0, 0)),
        ],
        out_specs=pl.BlockSpec((1, t, step_lanes), lambda b, p, i: (b, i, p)),
        out_shape=jax.ShapeDtypeStruct((batch, seq, D_MODEL), BF16),
        scratch_shapes=_attn_scratch(n_stats=2, groups=FOX_GROUPS),
        compiler_params=_params("parallel", "parallel", "arbitrary"),
        name="fox_attention",
    )(q, kv, kv, c_blocks)


def _first_index_of_max(x, lane, valid):
    masked = jnp.where(valid, x, -jnp.inf)
    top = jnp.max(masked, axis=1, keepdims=True)
    idx = jnp.min(jnp.where(masked == top, lane, LANES), axis=1, keepdims=True)
    return top, idx


def _router_kernel(h_ref, g_ref, whi_ref, wlo_ref, xn_ref, cmb_ref):
    xn = _rms_normed(h_ref[...], g_ref[...])
    x_hi = xn.astype(BF16)
    xn_ref[...] = x_hi
    x_lo = (xn - x_hi.astype(F32)).astype(BF16)
    logits = (jnp.dot(x_hi, whi_ref[...], preferred_element_type=F32)
              + jnp.dot(x_lo, whi_ref[...], preferred_element_type=F32)
              + jnp.dot(x_hi, wlo_ref[...], preferred_element_type=F32))
    lane = lax.broadcasted_iota(jnp.int32, logits.shape, 1)
    is_group = (lane >= N_EXPERTS) & (lane < N_EXPERTS + N_GROUPS)
    g_top, g_lane = _first_index_of_max(logits, lane, is_group)
    g_sum = jnp.sum(jnp.where(is_group, jnp.exp(logits - g_top), 0.0), axis=1, keepdims=True)
    g_prob = 1.0 / g_sum
    g_idx = g_lane - N_EXPERTS
    in_group = (lane >= g_idx * EXPERTS_PER_GROUP) & (lane < (g_idx + 1) * EXPERTS_PER_GROUP)
    t1, i1 = _first_index_of_max(logits, lane, in_group)
    t2, i2 = _first_index_of_max(logits, lane, in_group & (lane != i1))
    e2 = jnp.exp(t2 - t1)
    w1 = g_prob / (1.0 + e2)
    w2 = g_prob * e2 / (1.0 + e2)
    cmb_ref[...] = jnp.where(lane == i1, w1, 0.0) + jnp.where(lane == i2, w2, 0.0)


def _router(h, gain, w_hi, w_lo):
    n, d = h.shape
    return pl.pallas_call(
        _router_kernel,
        grid=(n // ROW_TILE,),
        in_specs=[
            pl.BlockSpec((ROW_TILE, d), lambda i: (i, 0)),
            pl.BlockSpec((1, d), lambda i: (0, 0)),
            pl.BlockSpec((d, LANES), lambda i: (0, 0)),
            pl.BlockSpec((d, LANES), lambda i: (0, 0)),
        ],
        out_specs=[
            pl.BlockSpec((ROW_TILE, d), lambda i: (i, 0)),
            pl.BlockSpec((ROW_TILE, LANES), lambda i: (i, 0)),
        ],
        out_shape=[
            jax.ShapeDtypeStruct((n, d), BF16),
            jax.ShapeDtypeStruct((n, LANES), F32),
        ],
        compiler_params=_params("parallel"),
        name="moe_router",
    )(h, gain.reshape(1, d), w_hi, w_lo)


def _moe_dense_kernel(xn_ref, cmb_ref, h_ref, wg_ref, wu_ref, wd_ref, fg_ref, o_ref, *, final_norm):
    e = pl.program_id(1)

    @pl.when(e == 0)
    def _():
        o_ref[...] = h_ref[...]

    x = xn_ref[...]
    gate = jnp.dot(x, wg_ref[0, 0].astype(BF16), preferred_element_type=F32)
    up = jnp.dot(x, wu_ref[0, 0].astype(BF16), preferred_element_type=F32)
    cmb = cmb_ref[...]
    lane = lax.broadcasted_iota(jnp.int32, cmb.shape, 1)
    scale = jnp.sum(jnp.where(lane == e, cmb, 0.0), axis=1, keepdims=True)
    act = gate / (1.0 + jnp.exp(-gate)) * up * scale
    o_ref[...] += jnp.dot(act.astype(BF16), wd_ref[0, 0].astype(BF16), preferred_element_type=F32)

    if final_norm:
        @pl.when(e == pl.num_programs(1) - 1)
        def _():
            o_ref[...] = _rms_normed(o_ref[...], fg_ref[...])


def _moe_dense(xn, cmb, h, layer, w_gate, w_up, w_down, final_gain):
    n, d = h.shape
    f = w_gate.shape[3]
    rows = MOE_ROW_TILE
    final_norm = final_gain is not None
    gain = final_gain if final_norm else jnp.ones((d,), F32)
    return pl.pallas_call(
        functools.partial(_moe_dense_kernel, final_norm=final_norm),
        grid=(n // rows, N_EXPERTS),
        in_specs=[
            pl.BlockSpec((rows, d), lambda i, e: (i, 0)),
            pl.BlockSpec((rows, LANES), lambda i, e: (i, 0)),
            pl.BlockSpec((rows, d), lambda i, e: (i, 0)),
            pl.BlockSpec((1, 1, d, f), lambda i, e: (layer, e, 0, 0)),
            pl.BlockSpec((1, 1, d, f), lambda i, e: (layer, e, 0, 0)),
            pl.BlockSpec((1, 1, f, d), lambda i, e: (layer, e, 0, 0)),
            pl.BlockSpec((1, d), lambda i, e: (0, 0)),
        ],
        out_specs=pl.BlockSpec((rows, d), lambda i, e: (i, 0)),
        out_shape=jax.ShapeDtypeStruct((n, d), F32),
        compiler_params=pltpu.CompilerParams(dimension_semantics=("parallel", "arbitrary"),
                                             vmem_limit_bytes=MOE_VMEM_LIMIT),
        name="moe_experts",
    )(xn, cmb, h, w_gate, w_up, w_down, gain.reshape(1, d))


def _moe_layer(h, gain, w_group, w_router, layer, w_gate, w_up, w_down, final_gain):
    d = h.shape[1]
    w_r = jnp.zeros((d, LANES), F32)
    w_r = w_r.at[:, :N_EXPERTS].set(w_router).at[:, N_EXPERTS:N_EXPERTS + N_GROUPS].set(w_group)
    w_hi = w_r.astype(BF16)
    w_lo = (w_r - w_hi.astype(F32)).astype(BF16)
    xn, cmb = _router(h, gain, w_hi, w_lo)
    return _moe_dense(xn, cmb, h, layer, w_gate, w_up, w_down, final_gain)


def kernel(x, attn_norm, w_qkv_a, w_o_a, kv_norm, w_kvf, b_f, w_q_b, w_o_b, moe_norm,
           w_group, w_router, w_gate, w_up, w_down, final_norm):
    batch, seq, d = x.shape
    n = batch * seq
    q_scale = HEAD_DIM ** -0.5 * LOG2_E
    h = x.reshape(n, d)
    n_a = w_qkv_a.shape[0]
    depth = attn_norm.shape[0]
    kv = c_blocks = None
    for layer in range(depth):
        if layer < n_a:
            w = w_qkv_a[layer]
            w = jnp.concatenate([w[:, :d] * q_scale, w[:, d:]], axis=1).astype(BF16)
            qkv = _norm_matmul(h, attn_norm[layer], w)
            mix = _sb_attention(qkv.reshape(batch, seq, 3 * d), batch, seq)
            h = _matmul_residual(mix.reshape(n, d), w_o_a[layer].astype(BF16), h)
        else:
            j = layer - n_a
            if j == 0:
                kv = _norm_matmul(h, kv_norm, w_kvf[:, :2 * d].astype(BF16))
                w_f_t = w_kvf[:, 2 * d:].T.astype(BF16)
                cum = _forget_gates(h.reshape(batch, seq, d), kv_norm, w_f_t, b_f)
                t = ATTN_TILE
                heads = FOX_GROUPS * HEADS_PER_GROUP
                c_blocks = cum.reshape(batch, N_HEADS // heads, heads, seq // t, t).transpose(0, 1, 3, 2, 4)
                kv = kv.reshape(batch, seq, 2 * d)
            q = _norm_matmul(h, attn_norm[layer], (w_q_b[j] * q_scale).astype(BF16))
            mix = _fox_attention(q.reshape(batch, seq, d), kv, c_blocks, batch, seq)
            h = _matmul_residual(mix.reshape(n, d), w_o_b[j].astype(BF16), h)
        final_gain = final_norm if layer == depth - 1 else None
        h = _moe_layer(h, moe_norm[layer], w_group[layer], w_router[layer], layer,
                       w_gate, w_up, w_down, final_gain)
    return h.reshape(batch, seq, d)
```

```python
import functools

import jax
import jax.numpy as jnp
from jax import lax
from jax.experimental import pallas as pl
from jax.experimental.pallas import tpu as pltpu

D_MODEL = 1024
N_HEADS = 16
HEAD_DIM = 64
N_GROUPS = 4
EXPERTS_PER_GROUP = 4
N_EXPERTS = 16
D_EXPERT = 512
RMS_EPS = 1e-6

LANES = 128
MXU_LANES = 256
HEADS_PER_GROUP = MXU_LANES // HEAD_DIM
SB_GROUPS = 4
FOX_GROUPS = 4
LOG2_E = 1.4426950408889634
SOFTPLUS_CLAMP = 64.0
ROW_TILE = 512
MOE_ROW_TILE = 1024
ATTN_TILE = 256
VMEM_LIMIT = 48 * 1024 * 1024
MOE_VMEM_LIMIT = 56 * 1024 * 1024

F32 = jnp.float32
BF16 = jnp.bfloat16
NT_DIMS = (((1,), (1,)), ((), ()))


def _params(*semantics):
    return pltpu.CompilerParams(dimension_semantics=semantics, vmem_limit_bytes=VMEM_LIMIT)


def _rms_normed(x, gain):
    var = jnp.mean(x * x, axis=-1, keepdims=True)
    return x * lax.rsqrt(var + RMS_EPS) * gain


def _norm_matmul_kernel(h_ref, g_ref, w_ref, o_ref):
    xn = _rms_normed(h_ref[...], g_ref[...]).astype(BF16)
    o_ref[...] = jnp.dot(xn, w_ref[...], preferred_element_type=F32).astype(o_ref.dtype)


def _norm_matmul(h, gain, w):
    n, d = h.shape
    n_out = w.shape[1]
    return pl.pallas_call(
        _norm_matmul_kernel,
        grid=(n // ROW_TILE,),
        in_specs=[
            pl.BlockSpec((ROW_TILE, d), lambda i: (i, 0)),
            pl.BlockSpec((1, d), lambda i: (0, 0)),
            pl.BlockSpec((d, n_out), lambda i: (0, 0)),
        ],
        out_specs=pl.BlockSpec((ROW_TILE, n_out), lambda i: (i, 0)),
        out_shape=jax.ShapeDtypeStruct((n, n_out), BF16),
        compiler_params=_params("parallel"),
        name="norm_matmul",
    )(h, gain.reshape(1, d), w)


def _matmul_residual_kernel(mix_ref, w_ref, h_ref, o_ref):
    o_ref[...] = h_ref[...] + jnp.dot(mix_ref[...], w_ref[...], preferred_element_type=F32)


def _matmul_residual(mix, w, h):
    n, d = h.shape
    return pl.pallas_call(
        _matmul_residual_kernel,
        grid=(n // ROW_TILE,),
        in_specs=[
            pl.BlockSpec((ROW_TILE, d), lambda i: (i, 0)),
            pl.BlockSpec((d, d), lambda i: (0, 0)),
            pl.BlockSpec((ROW_TILE, d), lambda i: (i, 0)),
        ],
        out_specs=pl.BlockSpec((ROW_TILE, d), lambda i: (i, 0)),
        out_shape=jax.ShapeDtypeStruct((n, d), F32),
        compiler_params=_params("parallel"),
        name="matmul_residual",
    )(mix, w, h)


def _head_ids(rows):
    lane = lax.broadcasted_iota(jnp.int32, (rows, MXU_LANES), 1)
    return lane // HEAD_DIM


def _only_head(x, head_ids, h):
    return jnp.where(head_ids == h, x, jnp.zeros_like(x))


def _group_lanes(g):
    return slice(g * MXU_LANES, (g + 1) * MXU_LANES)


def _masked_queries(q_ref, head_ids, groups):
    return [_only_head(q_ref[0, :, _group_lanes(g)], head_ids, h)
            for g in range(groups) for h in range(HEADS_PER_GROUP)]


def _key_scores(q_heads, k_ref, rows):
    out = []
    for g in range(len(q_heads) // HEADS_PER_GROUP):
        kb = k_ref[0, rows, _group_lanes(g)]
        for h in range(HEADS_PER_GROUP):
            out.append(lax.dot_general(q_heads[g * HEADS_PER_GROUP + h], kb, NT_DIMS,
                                       preferred_element_type=F32))
    return out


def _weighted_values(ws_ref, v_ref, rows, head_ids):
    out = []
    for g in range(ws_ref.shape[0] // HEADS_PER_GROUP):
        vb = v_ref[0, rows, _group_lanes(g)]
        acc = None
        for h in range(HEADS_PER_GROUP):
            pv = jnp.dot(ws_ref[g * HEADS_PER_GROUP + h], _only_head(vb, head_ids, h),
                         preferred_element_type=F32)
            acc = pv if acc is None else acc + pv
        out.append(acc)
    return out


def _attn_scratch(n_stats, groups):
    t = ATTN_TILE
    heads = groups * HEADS_PER_GROUP
    return ([pltpu.VMEM((heads, t, t), BF16)]
            + [pltpu.VMEM((heads, t, LANES), F32)] * n_stats
            + [pltpu.VMEM((groups, t, MXU_LANES), F32)])


def _lane_tiled(x, width):
    return jnp.tile(x, (1, width // LANES))


def _per_head_lanes(head_ids, cols):
    cols = [_lane_tiled(c, MXU_LANES) for c in cols]
    out = cols[-1]
    for h in range(HEADS_PER_GROUP - 2, -1, -1):
        out = jnp.where(head_ids == h, cols[h], out)
    return out


def _block_rows(j):
    return pl.ds(pl.multiple_of(j * ATTN_TILE, ATTN_TILE), ATTN_TILE)


def _sb_attn_kernel(q_ref, k_ref, v_ref, o_ref, ws_ref, cs_ref, acc_ref):
    t = ATTN_TILE
    heads, groups = ws_ref.shape[0], acc_ref.shape[0]
    i = pl.program_id(2)
    head_ids = _head_ids(t)
    q_heads = _masked_queries(q_ref, head_ids, groups)
    row = lax.broadcasted_iota(jnp.int32, (t, t), 0)
    col = lax.broadcasted_iota(jnp.int32, (t, t), 1)
    strict = col < row
    neg_suffix_ones = jnp.where(row >= col, -1.0, 0.0).astype(BF16)

    def weights(j, diagonal):
        cs = [cs_ref[h] for h in range(heads)]
        ws = []
        for h, z in enumerate(_key_scores(q_heads, k_ref, _block_rows(j))):
            sp = jnp.maximum(z, jnp.log(1.0 + jnp.exp2(jnp.minimum(z, SOFTPLUS_CLAMP))) * LOG2_E)
            if diagonal:
                sp = jnp.where(strict, sp, 0.0)
            suffix = jnp.dot(sp.astype(BF16), neg_suffix_ones, preferred_element_type=F32)
            w = jnp.exp2(z + suffix + _lane_tiled(cs[h], t))
            if diagonal:
                w = jnp.where(strict, w, 0.0)
            ws.append(w.astype(BF16))
            cs[h] = cs[h] - jnp.sum(sp, axis=1, keepdims=True)
        for h in range(heads):
            ws_ref[h] = ws[h]
            cs_ref[h] = cs[h]

    def add_pv(j):
        for g, pv in enumerate(_weighted_values(ws_ref, v_ref, _block_rows(j), head_ids)):
            acc_ref[g] += pv

    def trip(n, carry):
        j = i - 1 - n
        add_pv(j + 1)
        weights(j, False)
        return carry

    cs_ref[...] = jnp.zeros(cs_ref.shape, F32)
    acc_ref[...] = jnp.zeros(acc_ref.shape, F32)
    weights(i, True)
    lax.fori_loop(0, i, trip, 0)
    add_pv(0)
    for g in range(groups):
        o_ref[0, :, _group_lanes(g)] = acc_ref[g].astype(o_ref.dtype)


def _sb_attention(qkv, batch, seq):
    t = ATTN_TILE
    step_lanes = SB_GROUPS * MXU_LANES
    k_off = D_MODEL // step_lanes
    return pl.pallas_call(
        _sb_attn_kernel,
        grid=(batch, D_MODEL // step_lanes, seq // t),
        in_specs=[
            pl.BlockSpec((1, t, step_lanes), lambda b, p, i: (b, i, p)),
            pl.BlockSpec((1, seq, step_lanes), lambda b, p, i: (b, 0, k_off + p)),
            pl.BlockSpec((1, seq, step_lanes), lambda b, p, i: (b, 0, 2 * k_off + p)),
        ],
        out_specs=pl.BlockSpec((1, t, step_lanes), lambda b, p, i: (b, i, p)),
        out_shape=jax.ShapeDtypeStruct((batch, seq, D_MODEL), BF16),
        scratch_shapes=_attn_scratch(n_stats=1, groups=SB_GROUPS),
        compiler_params=_params("parallel", "parallel", "arbitrary"),
        name="sb_attention",
    )(qkv, qkv, qkv)


def _forget_gate_kernel(h_ref, g_ref, wf_ref, bf_ref, o_ref):
    t = ATTN_TILE
    seq = h_ref.shape[1]
    u = _rms_normed(h_ref[0], g_ref[...]).astype(BF16)
    logits = lax.dot_general(wf_ref[...], u, NT_DIMS, preferred_element_type=F32) + bf_ref[...]
    log_f = jnp.minimum(logits, 0.0) - jnp.log(1.0 + jnp.exp(-jnp.abs(logits)))
    row = lax.broadcasted_iota(jnp.int32, (t, t), 0)
    col = lax.broadcasted_iota(jnp.int32, (t, t), 1)
    prefix_ones = (row <= col).astype(BF16)
    carry = jnp.zeros((N_HEADS, 1), F32)
    for n in range(seq // t):
        blk = log_f[:, n * t:(n + 1) * t]
        hi = blk.astype(BF16)
        lo = (blk - hi.astype(F32)).astype(BF16)
        cs = (jnp.dot(hi, prefix_ones, preferred_element_type=F32)
              + jnp.dot(lo, prefix_ones, preferred_element_type=F32) + carry)
        o_ref[0, :, n * t:(n + 1) * t] = cs * LOG2_E
        carry = cs[:, t - 1:t]


def _forget_gates(h3, gain, w_f_t, b_f):
    batch, seq, d = h3.shape
    return pl.pallas_call(
        _forget_gate_kernel,
        grid=(batch,),
        in_specs=[
            pl.BlockSpec((1, seq, d), lambda b: (b, 0, 0)),
            pl.BlockSpec((1, d), lambda b: (0, 0)),
            pl.BlockSpec((N_HEADS, d), lambda b: (0, 0)),
            pl.BlockSpec((N_HEADS, 1), lambda b: (0, 0)),
        ],
        out_specs=pl.BlockSpec((1, N_HEADS, seq), lambda b: (b, 0, 0)),
        out_shape=jax.ShapeDtypeStruct((batch, N_HEADS, seq), F32),
        compiler_params=_params("parallel"),
        name="forget_gates",
    )(h3, gain.reshape(1, d), w_f_t, b_f.reshape(N_HEADS, 1))


def _fox_attn_kernel(q_ref, k_ref, v_ref, c_ref, o_ref, ps_ref, ms_ref, ls_ref, acc_ref):
    t = ATTN_TILE
    heads, groups = ps_ref.shape[0], acc_ref.shape[0]
    i = pl.program_id(2)
    head_ids = _head_ids(t)
    q_heads = _masked_queries(q_ref, head_ids, groups)
    row = lax.broadcasted_iota(jnp.int32, (t, t), 0)
    col = lax.broadcasted_iota(jnp.int32, (t, t), 1)
    causal = col <= row

    def probs(j, diagonal):
        cb = c_ref[0, 0, j]
        ms = [ms_ref[h] for h in range(heads)]
        ls = [ls_ref[h] for h in range(heads)]
        ps, alphas = [], []
        for h, qk in enumerate(_key_scores(q_heads, k_ref, _block_rows(j))):
            s = qk - cb[h:h + 1, :]
            if diagonal:
                s = jnp.where(causal, s, -jnp.inf)
            m_new = jnp.maximum(ms[h], jnp.max(s, axis=1, keepdims=True))
            p = jnp.exp2(s - _lane_tiled(m_new, t))
            alpha = jnp.exp2(ms[h] - m_new)
            ps.append(p.astype(BF16))
            ms[h] = m_new
            ls[h] = alpha * ls[h] + jnp.sum(p, axis=1, keepdims=True)
            alphas.append(alpha)
        for h in range(heads):
            ps_ref[h] = ps[h]
            ms_ref[h] = ms[h]
            ls_ref[h] = ls[h]
        return alphas

    def group_lanes_of(cols):
        return [_per_head_lanes(head_ids, cols[g * HEADS_PER_GROUP:(g + 1) * HEADS_PER_GROUP])
                for g in range(groups)]

    def trip(n, carry):
        j = i - 1 - n
        pvs = _weighted_values(ps_ref, v_ref, _block_rows(j + 1), head_ids)
        alphas = probs(j, False)
        for g, a in enumerate(group_lanes_of(alphas)):
            acc_ref[g] = (acc_ref[g] + pvs[g]) * a
        return carry

    ms_ref[...] = jnp.full(ms_ref.shape, -jnp.inf, F32)
    ls_ref[...] = jnp.zeros(ls_ref.shape, F32)
    acc_ref[...] = jnp.zeros(acc_ref.shape, F32)
    probs(i, True)
    lax.fori_loop(0, i, trip, 0)
    pvs = _weighted_values(ps_ref, v_ref, _block_rows(0), head_ids)
    denoms = group_lanes_of([ls_ref[h] for h in range(heads)])
    for g in range(groups):
        o_ref[0, :, _group_lanes(g)] = ((acc_ref[g] + pvs[g]) / denoms[g]).astype(o_ref.dtype)


def _fox_attention(q, kv, c_blocks, batch, seq):
    t = ATTN_TILE
    step_lanes = FOX_GROUPS * MXU_LANES
    v_off = D_MODEL // step_lanes
    return pl.pallas_call(
        _fox_attn_kernel,
        grid=(batch, D_MODEL // step_lanes, seq // t),
        in_specs=[
            pl.BlockSpec((1, t, step_lanes), lambda b, p, i: (b, i, p)),
            pl.BlockSpec((1, seq, step_lanes), lambda b, p, i: (b, 0, p)),
            pl.BlockSpec((1, seq, step_lanes), lambda b, p, i: (b, 0, v_off + p)),
            pl.BlockSpec((1, 1, seq // t, FOX_GROUPS * HEADS_PER_GROUP, t), lambda b, p, i: (b, p, 0, 0, 0)),
        ],
        out_specs=pl.BlockSpec((1, t, step_lanes), lambda b, p, i: (b, i, p)),
        out_shape=jax.ShapeDtypeStruct((batch, seq, D_MODEL), BF16),
        scratch_shapes=_attn_scratch(n_stats=2, groups=FOX_GROUPS),
        compiler_params=_params("parallel", "parallel", "arbitrary"),
        name="fox_attention",
    )(q, kv, kv, c_blocks)


def _first_index_of_max(x, lane, valid):
    masked = jnp.where(valid, x, -jnp.inf)
    top = jnp.max(masked, axis=1, keepdims=True)
    idx = jnp.min(jnp.where(masked == top, lane, LANES), axis=1, keepdims=True)
    return top, idx


def _router_kernel(h_ref, g_ref, whi_ref, wlo_ref, xn_ref, cmb_ref):
    xn = _rms_normed(h_ref[...], g_ref[...])
    x_hi = xn.astype(BF16)
    xn_ref[...] = x_hi
    x_lo = (xn - x_hi.astype(F32)).astype(BF16)
    logits = (jnp.dot(x_hi, whi_ref[...], preferred_element_type=F32)
              + jnp.dot(x_lo, whi_ref[...], preferred_element_type=F32)
              + jnp.dot(x_hi, wlo_ref[...], preferred_element_type=F32))
    lane = lax.broadcasted_iota(jnp.int32, logits.shape, 1)
    is_group = (lane >= N_EXPERTS) & (lane < N_EXPERTS + N_GROUPS)
    g_top, g_lane = _first_index_of_max(logits, lane, is_group)
    g_sum = jnp.sum(jnp.where(is_group, jnp.exp(logits - g_top), 0.0), axis=1, keepdims=True)
    g_prob = 1.0 / g_sum
    g_idx = g_lane - N_EXPERTS
    in_group = (lane >= g_idx * EXPERTS_PER_GROUP) & (lane < (g_idx + 1) * EXPERTS_PER_GROUP)
    t1, i1 = _first_index_of_max(logits, lane, in_group)
    t2, i2 = _first_index_of_max(logits, lane, in_group & (lane != i1))
    e2 = jnp.exp(t2 - t1)
    w1 = g_prob / (1.0 + e2)
    w2 = g_prob * e2 / (1.0 + e2)
    cmb_ref[...] = jnp.where(lane == i1, w1, 0.0) + jnp.where(lane == i2, w2, 0.0)


def _router(h, gain, w_hi, w_lo):
    n, d = h.shape
    return pl.pallas_call(
        _router_kernel,
        grid=(n // ROW_TILE,),
        in_specs=[
            pl.BlockSpec((ROW_TILE, d), lambda i: (i, 0)),
            pl.BlockSpec((1, d), lambda i: (0, 0)),
            pl.BlockSpec((d, LANES), lambda i: (0, 0)),
            pl.BlockSpec((d, LANES), lambda i: (0, 0)),
        ],
        out_specs=[
            pl.BlockSpec((ROW_TILE, d), lambda i: (i, 0)),
            pl.BlockSpec((ROW_TILE, LANES), lambda i: (i, 0)),
        ],
        out_shape=[
            jax.ShapeDtypeStruct((n, d), BF16),
            jax.ShapeDtypeStruct((n, LANES), F32),
        ],
        compiler_params=_params("parallel"),
        name="moe_router",
    )(h, gain.reshape(1, d), w_hi, w_lo)


def _moe_dense_kernel(xn_ref, cmb_ref, h_ref, wg_ref, wu_ref, wd_ref, fg_ref, o_ref, *, final_norm):
    e = pl.program_id(1)

    @pl.when(e == 0)
    def _():
        o_ref[...] = h_ref[...]

    x = xn_ref[...]
    gate = jnp.dot(x, wg_ref[0, 0].astype(BF16), preferred_element_type=F32)
    up = jnp.dot(x, wu_ref[0, 0].astype(BF16), preferred_element_type=F32)
    cmb = cmb_ref[...]
    lane = lax.broadcasted_iota(jnp.int32, cmb.shape, 1)
    scale = jnp.sum(jnp.where(lane == e, cmb, 0.0), axis=1, keepdims=True)
    act = gate / (1.0 + jnp.exp(-gate)) * up * scale
    o_ref[...] += jnp.dot(act.astype(BF16), wd_ref[0, 0].astype(BF16), preferred_element_type=F32)

    if final_norm:
        @pl.when(e == pl.num_programs(1) - 1)
        def _():
            o_ref[...] = _rms_normed(o_ref[...], fg_ref[...])


def _moe_dense(xn, cmb, h, layer, w_gate, w_up, w_down, final_gain):
    n, d = h.shape
    f = w_gate.shape[3]
    rows = MOE_ROW_TILE
    final_norm = final_gain is not None
    gain = final_gain if final_norm else jnp.ones((d,), F32)
    return pl.pallas_call(
        functools.partial(_moe_dense_kernel, final_norm=final_norm),
        grid=(n // rows, N_EXPERTS),
        in_specs=[
            pl.BlockSpec((rows, d), lambda i, e: (i, 0)),
            pl.BlockSpec((rows, LANES), lambda i, e: (i, 0)),
            pl.BlockSpec((rows, d), lambda i, e: (i, 0)),
            pl.BlockSpec((1, 1, d, f), lambda i, e: (layer, e, 0, 0)),
            pl.BlockSpec((1, 1, d, f), lambda i, e: (layer, e, 0, 0)),
            pl.BlockSpec((1, 1, f, d), lambda i, e: (layer, e, 0, 0)),
            pl.BlockSpec((1, d), lambda i, e: (0, 0)),
        ],
        out_specs=pl.BlockSpec((rows, d), lambda i, e: (i, 0)),
        out_shape=jax.ShapeDtypeStruct((n, d), F32),
        compiler_params=pltpu.CompilerParams(dimension_semantics=("parallel", "arbitrary"),
                                             vmem_limit_bytes=MOE_VMEM_LIMIT),
        name="moe_experts",
    )(xn, cmb, h, w_gate, w_up, w_down, gain.reshape(1, d))


def _moe_layer(h, gain, w_group, w_router, layer, w_gate, w_up, w_down, final_gain):
    d = h.shape[1]
    w_r = jnp.zeros((d, LANES), F32)
    w_r = w_r.at[:, :N_EXPERTS].set(w_router).at[:, N_EXPERTS:N_EXPERTS + N_GROUPS].set(w_group)
    w_hi = w_r.astype(BF16)
    w_lo = (w_r - w_hi.astype(F32)).astype(BF16)
    xn, cmb = _router(h, gain, w_hi, w_lo)
    return _moe_dense(xn, cmb, h, layer, w_gate, w_up, w_down, final_gain)


def kernel(x, attn_norm, w_qkv_a, w_o_a, kv_norm, w_kvf, b_f, w_q_b, w_o_b, moe_norm,
           w_group, w_router, w_gate, w_up, w_down, final_norm):
    batch, seq, d = x.shape
    n = batch * seq
    q_scale = HEAD_DIM ** -0.5 * LOG2_E
    h = x.reshape(n, d)
    n_a = w_qkv_a.shape[0]
    depth = attn_norm.shape[0]
    kv = c_blocks = None
    for layer in range(depth):
        if layer < n_a:
            w = w_qkv_a[layer]
            w = jnp.concatenate([w[:, :d] * q_scale, w[:, d:]], axis=1).astype(BF16)
            qkv = _norm_matmul(h, attn_norm[layer], w)
            mix = _sb_attention(qkv.reshape(batch, seq, 3 * d), batch, seq)
            h = _matmul_residual(mix.reshape(n, d), w_o_a[layer].astype(BF16), h)
        else:
            j = layer - n_a
            if j == 0:
                kv = _norm_matmul(h, kv_norm, w_kvf[:, :2 * d].astype(BF16))
                w_f_t = w_kvf[:, 2 * d:].T.astype(BF16)
                cum = _forget_gates(h.reshape(batch, seq, d), kv_norm, w_f_t, b_f)
                t = ATTN_TILE
                heads = FOX_GROUPS * HEADS_PER_GROUP
                c_blocks = cum.reshape(batch, N_HEADS // heads, heads, seq // t, t).transpose(0, 1, 3, 2, 4)
                kv = kv.reshape(batch, seq, 2 * d)
            q = _norm_matmul(h, attn_norm[layer], (w_q_b[j] * q_scale).astype(BF16))
            mix = _fox_attention(q.reshape(batch, seq, d), kv, c_blocks, batch, seq)
            h = _matmul_residual(mix.reshape(n, d), w_o_b[j].astype(BF16), h)
        final_gain = final_norm if layer == depth - 1 else None
        h = _moe_layer(h, moe_norm[layer], w_group[layer], w_router[layer], layer,
                       w_gate, w_up, w_down, final_gain)
    return h.reshape(batch, seq, d)
```
